```python
import jax, jax.numpy as jnp
from jax import lax
import numpy as np

D_MODEL = 2048
BATCH = 4
SEQ = 2048
DEPTH = 1
DEC_BATCH = 128
DEC_SEQ = 8
PAST_LEN = 16384
PAGE_SIZE = 128

D_CONV = D_MODEL // 2
CONV_WIDTH = 3
N_HEADS = 4
D_QK = D_MODEL // 8
D_V = D_MODEL // 4
D_QKB = N_HEADS * D_QK
D_VB = N_HEADS * D_V
CHUNK = 128
FORGET_BIAS = 3.0
N_GROUPS = 4
EXPERTS_PER_GROUP = 4
N_EXPERTS = N_GROUPS * EXPERTS_PER_GROUP
TOP_K = 2
D_EXPERT = D_MODEL // 2
EPS = 1e-6
SECTION_SIZES = (D_CONV, D_CONV, D_CONV, D_QKB, D_QKB, D_VB, D_VB, N_HEADS, N_HEADS, D_MODEL, D_MODEL)
D_IN = 3 * D_CONV + 2 * D_QKB + 2 * D_VB + 2 * N_HEADS + 2 * D_MODEL

kernel_name = 'hybrid_conv_mlstm_hmoe_step'


def rmsnorm(x, g):
    xf = x.astype(jnp.float32)
    r = lax.rsqrt(jnp.mean(xf * xf, axis=-1, keepdims=True) + EPS)
    return (xf * r * g.astype(jnp.float32)).astype(x.dtype)


def short_conv(u, buf, w):
    T = u.shape[1]
    up = jnp.concatenate([buf.astype(u.dtype), u], axis=1)
    y = w[0] * up[:, 0:T]
    for j in range(1, CONV_WIDTH):
        y = y + w[j] * up[:, j:j + T]
    new_buf = up[:, up.shape[1] - (CONV_WIDTH - 1):]
    return y, new_buf


def mlstm_chunk(carry, inp):
    C, n, m = carry
    q, k, v, ig, lf = inp
    L = q.shape[2]
    causal = jnp.tril(jnp.ones((L, L), dtype=bool))
    b = jnp.cumsum(lf, axis=-1)
    log_inter = b + m[..., None]
    dmat = b[..., :, None] - b[..., None, :] + ig[..., None, :]
    dmat = jnp.where(causal, dmat, -jnp.inf)
    m_tok = jnp.maximum(log_inter, jnp.max(dmat, axis=-1))
    w_inter = jnp.exp(log_inter - m_tok)
    s = jnp.exp(dmat - m_tok[..., None]) * jnp.einsum('bhld,bhsd->bhls', q, k)
    num = w_inter[..., None] * jnp.einsum('bhld,bhdv->bhlv', q, C) + jnp.einsum('bhls,bhsv->bhlv', s, v)
    den = w_inter * jnp.einsum('bhld,bhd->bhl', q, n) + jnp.sum(s, axis=-1)
    h = num / jnp.maximum(jnp.abs(den), jnp.exp(-m_tok))[..., None]
    b_last = b[..., -1]
    log_kv = b_last[..., None] - b + ig
    m_new = jnp.maximum(b_last + m, jnp.max(log_kv, axis=-1))
    f_st = jnp.exp(b_last + m - m_new)
    w_kv = jnp.exp(log_kv - m_new[..., None])
    C_new = f_st[..., None, None] * C + jnp.einsum('bhl,bhld,bhlv->bhdv', w_kv, k, v)
    n_new = f_st[..., None] * n + jnp.einsum('bhl,bhld->bhd', w_kv, k)
    return (C_new, n_new, m_new), h


def mlstm(q, k, v, ig, lf, C0, n0, m0):
    Bsz, T = q.shape[0], q.shape[1]
    L = CHUNK if T % CHUNK == 0 else T
    nc = T // L

    def to_chunks(a):
        a = a.reshape((Bsz, nc, L) + a.shape[2:])
        return jnp.moveaxis(jnp.moveaxis(a, 3, 2), 1, 0)

    xs = (to_chunks(q), to_chunks(k), to_chunks(v), to_chunks(ig), to_chunks(lf))
    (C, n, m), h = lax.scan(mlstm_chunk, (C0, n0, m0), xs)
    h = jnp.moveaxis(jnp.moveaxis(h, 0, 1), 2, 3).reshape(Bsz, T, N_HEADS, D_V)
    return h, C, n, m


def hier_moe(h, w_grp, w_rt, w_gate, w_up, w_down):
    Bsz, T = h.shape[0], h.shape[1]
    gl = jnp.einsum('btd,dg->btg', h, w_grp).astype(jnp.float32)
    g_star = jnp.argmax(gl, axis=-1)
    p_grp = jnp.take_along_axis(jax.nn.softmax(gl, axis=-1), g_star[..., None], axis=-1)
    el = jnp.einsum('btd,de->bte', h, w_rt).astype(jnp.float32).reshape(Bsz, T, N_GROUPS, EXPERTS_PER_GROUP)
    el_g = jnp.take_along_axis(el, g_star[..., None, None], axis=2)[..., 0, :]
    pe = jax.nn.softmax(el_g, axis=-1)
    top_v, top_i = lax.top_k(pe, TOP_K)
    top_v = top_v / jnp.sum(top_v, axis=-1, keepdims=True) * p_grp
    eid = g_star[..., None] * EXPERTS_PER_GROUP + top_i
    gates = jnp.sum(jax.nn.one_hot(eid, N_EXPERTS, dtype=jnp.float32) * top_v[..., None], axis=-2)
    gates = gates.astype(h.dtype)
    y = jnp.zeros_like(h)
    for e in range(N_EXPERTS):
        a = jnp.einsum('btd,df->btf', h, w_gate[e])
        u = jnp.einsum('btd,df->btf', h, w_up[e])
        y = y + gates[..., e:e + 1] * jnp.einsum('btf,fd->btd', jax.nn.silu(a) * u, w_down[e])
    return y


def layer(x, conv_buf, C0, n0, m0, norm1, w_in, b_if, conv_w, head_norm, w_a_out, w_b_out,
          w_out, norm2, w_grp, w_rt, w_gate, w_up, w_down):
    Bsz, T = x.shape[0], x.shape[1]
    hn = rmsnorm(x, norm1)
    z = jnp.einsum('btd,de->bte', hn, w_in)
    offsets = [int(o) for o in np.cumsum(SECTION_SIZES)[:-1]]
    cb, cc, cx, zq, zk, zv, zo, zi, zf, ga, gb = jnp.split(z, offsets, axis=-1)
    yc, new_buf = short_conv(cc * cx, conv_buf, conv_w)
    ya = jnp.einsum('btc,cd->btd', cb * yc, w_a_out)
    f32 = jnp.float32
    q = zq.reshape(Bsz, T, N_HEADS, D_QK).astype(f32) * (D_QK ** -0.5)
    k = zk.reshape(Bsz, T, N_HEADS, D_QK).astype(f32)
    v = zv.reshape(Bsz, T, N_HEADS, D_V).astype(f32)
    bif = b_if.astype(f32)
    ig = zi.astype(f32) + bif[:N_HEADS]
    lf = jax.nn.log_sigmoid(zf.astype(f32) + bif[N_HEADS:])
    hb, C, n, m = mlstm(q, k, v, ig, lf, C0.astype(f32), n0.astype(f32), m0.astype(f32))
    hb = hb * lax.rsqrt(jnp.mean(hb * hb, axis=-1, keepdims=True) + EPS) * head_norm.astype(f32).reshape(N_HEADS, D_V)
    hb = jax.nn.sigmoid(zo) * hb.reshape(Bsz, T, D_VB).astype(x.dtype)
    yb = jnp.einsum('btv,vd->btd', hb, w_b_out)
    mix = jax.nn.sigmoid(ga) * ya + jax.nn.sigmoid(gb) * yb
    x = x + jnp.einsum('btd,de->bte', mix, w_out)
    x = x + hier_moe(rmsnorm(x, norm2), w_grp, w_rt, w_gate, w_up, w_down)
    return x, new_buf, C, n, m


def setup_inputs(seed: int = 0) -> dict:
    key = jax.random.key(seed)
    ks = jax.random.split(key, 24)

    def nrm(k, shape, s):
        return jax.random.normal(k, shape, jnp.float32) * s

    b_i = nrm(ks[8], (DEPTH, N_HEADS), 0.1)
    b_f = FORGET_BIAS + nrm(ks[20], (DEPTH, N_HEADS), 0.1)
    return {
        'x_prompt': nrm(ks[0], (BATCH, SEQ, D_MODEL), 1.0),
        'x_sample': nrm(ks[1], (DEC_BATCH, DEC_SEQ, D_MODEL), 1.0),
        'state_conv': nrm(ks[2], (DEPTH, DEC_BATCH, CONV_WIDTH - 1, D_CONV), 1.0),
        'state_C': nrm(ks[3], (DEPTH, DEC_BATCH, N_HEADS, D_QK, D_V), 0.05),
        'state_n': jnp.abs(nrm(ks[4], (DEPTH, DEC_BATCH, N_HEADS, D_QK), 1.0)),
        'state_m': nrm(ks[5], (DEPTH, DEC_BATCH, N_HEADS), 1.0),
        'norm1': 1.0 + nrm(ks[6], (DEPTH, D_MODEL), 0.02),
        'w_in': nrm(ks[7], (DEPTH, D_MODEL, D_IN), D_MODEL ** -0.5),
        'b_if': jnp.concatenate([b_i, b_f], axis=-1),
        'conv_w': nrm(ks[9], (DEPTH, CONV_WIDTH, D_CONV), CONV_WIDTH ** -0.5),
        'head_norm': 1.0 + nrm(ks[10], (DEPTH, D_VB), 0.02),
        'w_a_out': nrm(ks[11], (DEPTH, D_CONV, D_MODEL), D_CONV ** -0.5),
        'w_b_out': nrm(ks[12], (DEPTH, D_VB, D_MODEL), D_VB ** -0.5),
        'w_out': nrm(ks[13], (DEPTH, D_MODEL, D_MODEL), D_MODEL ** -0.5),
        'norm2': 1.0 + nrm(ks[14], (DEPTH, D_MODEL), 0.02),
        'w_grp': nrm(ks[15], (DEPTH, D_MODEL, N_GROUPS), D_MODEL ** -0.5),
        'w_rt': nrm(ks[16], (DEPTH, D_MODEL, N_EXPERTS), D_MODEL ** -0.5),
        'w_expert_gate': nrm(ks[17], (DEPTH, N_EXPERTS, D_MODEL, D_EXPERT), D_MODEL ** -0.5),
        'w_expert_up': nrm(ks[18], (DEPTH, N_EXPERTS, D_MODEL, D_EXPERT), D_MODEL ** -0.5),
        'w_expert_down': nrm(ks[19], (DEPTH, N_EXPERTS, D_EXPERT, D_MODEL), D_EXPERT ** -0.5),
        'norm_final': 1.0 + nrm(ks[21], (D_MODEL,), 0.02),
    }


def reference(x_prompt, x_sample, state_conv, state_C, state_n, state_m, norm1, w_in, b_if,
              conv_w, head_norm, w_a_out, w_b_out, w_out, norm2, w_grp, w_rt, w_expert_gate,
              w_expert_up, w_expert_down, norm_final):
    f32 = jnp.float32
    xp, xs = x_prompt, x_sample
    conv_p, C_p, n_p, m_p = [], [], [], []
    conv_s, C_s, n_s, m_s = [], [], [], []
    for l in range(DEPTH):
        w = (norm1[l], w_in[l], b_if[l], conv_w[l], head_norm[l], w_a_out[l], w_b_out[l], w_out[l],
             norm2[l], w_grp[l], w_rt[l], w_expert_gate[l], w_expert_up[l], w_expert_down[l])
        buf0 = jnp.zeros((BATCH, CONV_WIDTH - 1, D_CONV), xp.dtype)
        C0 = jnp.zeros((BATCH, N_HEADS, D_QK, D_V), f32)
        n0 = jnp.zeros((BATCH, N_HEADS, D_QK), f32)
        m0 = jnp.zeros((BATCH, N_HEADS), f32)
        xp, bp, cp, np_, mp = layer(xp, buf0, C0, n0, m0, *w)
        conv_p.append(bp); C_p.append(cp); n_p.append(np_); m_p.append(mp)
        xs, bs, cs, ns, ms = layer(xs, state_conv[l], state_C[l], state_n[l], state_m[l], *w)
        conv_s.append(bs); C_s.append(cs); n_s.append(ns); m_s.append(ms)
    y_prompt = rmsnorm(xp, norm_final)
    y_sample = rmsnorm(xs, norm_final)
    return (y_prompt, y_sample,
            jnp.stack(conv_p, 0), jnp.stack(C_p, 0), jnp.stack(n_p, 0), jnp.stack(m_p, 0),
            jnp.stack(conv_s, 0), jnp.stack(C_s, 0), jnp.stack(n_s, 0), jnp.stack(m_s, 0))
```

```python
import functools

import jax
import jax.numpy as jnp
from jax import lax
from jax.experimental import pallas as pl
from jax.experimental.pallas import tpu as pltpu

F32 = jnp.float32
BF16 = jnp.bfloat16

D_MODEL = 2048
BATCH = 4
SEQ = 2048
DEC_BATCH = 128
DEC_SEQ = 8
D_CONV = 1024
N_HEADS = 4
D_QK = 256
D_V = 512
D_QKB = N_HEADS * D_QK
D_VB = N_HEADS * D_V
CHUNK = 128
N_GROUPS = 4
EXPERTS_PER_GROUP = 4
N_EXPERTS = 16
TOP_K = 2
D_EXPERT = 1024
EPS = 1e-6

T_P = BATCH * SEQ
T_S = DEC_BATCH * DEC_SEQ
T_ALL = T_P + T_S
N_MAIN = 3 * D_CONV + 2 * D_QKB + 2 * D_VB
N_IF = 2 * N_HEADS
LANES = 128
VMEM_LIMIT = 56 * 1024 * 1024

TM_NORM = 512
TM_PROJ = 1024
TN_PROJ = 1024
TM_MIX = 256
TM_EXP = 256
N_ASSIGN = T_ALL * TOP_K
NT_EXP = N_ASSIGN // TM_EXP + N_EXPERTS
R_EXP = NT_EXP * TM_EXP
SAMPLE_SEQS_PER_STEP = 2


def _params(n_axes):
    return pltpu.CompilerParams(
        dimension_semantics=("arbitrary",) * n_axes, vmem_limit_bytes=VMEM_LIMIT)


def _split3(x):
    x1 = x.astype(BF16)
    r1 = x - x1.astype(F32)
    x2 = r1.astype(BF16)
    x3 = (r1 - x2.astype(F32)).astype(BF16)
    return x1, x2, x3


def _dot(a, b):
    return jnp.dot(a, b, preferred_element_type=F32)


def _dot_nt(a, b):
    return lax.dot_general(a, b, (((1,), (1,)), ((), ())), preferred_element_type=F32)


def _dot_tn(a, b):
    return lax.dot_general(a, b, (((0,), (0,)), ((), ())), preferred_element_type=F32)


def _norm_kernel(xp_ref, xs_ref, g_ref, o_ref):
    i = pl.program_id(0)

    def emit(x):
        r = lax.rsqrt(jnp.mean(x * x, axis=-1, keepdims=True) + EPS)
        o_ref[...] = (x * r * g_ref[...]).astype(o_ref.dtype)

    @pl.when(i < T_P // TM_NORM)
    def _():
        emit(xp_ref[...])

    @pl.when(i >= T_P // TM_NORM)
    def _():
        emit(xs_ref[...])


def _norm_call(xp, xs, g):
    npt = T_P // TM_NORM
    return pl.pallas_call(
        _norm_kernel,
        grid=(T_ALL // TM_NORM,),
        in_specs=[
            pl.BlockSpec((TM_NORM, D_MODEL), lambda i: (jnp.minimum(i, npt - 1), 0)),
            pl.BlockSpec((TM_NORM, D_MODEL), lambda i: (jnp.maximum(i - npt, 0), 0)),
            pl.BlockSpec((1, D_MODEL), lambda i: (0, 0)),
        ],
        out_specs=pl.BlockSpec((TM_NORM, D_MODEL), lambda i: (i, 0)),
        out_shape=jax.ShapeDtypeStruct((T_ALL, D_MODEL), BF16),
        compiler_params=_params(1),
        name="norm",
    )(xp, xs, g)


def _proj_kernel(h_ref, w_ref, o_ref, wb_ref):
    @pl.when(pl.program_id(1) == 0)
    def _():
        wb_ref[...] = w_ref[...].astype(BF16)

    o_ref[...] = _dot(h_ref[...], wb_ref[...]).astype(o_ref.dtype)


def _proj_call(h, w, n_cols, tn, out_dtype, name):
    return pl.pallas_call(
        _proj_kernel,
        grid=(n_cols // tn, T_ALL // TM_PROJ),
        in_specs=[
            pl.BlockSpec((TM_PROJ, D_MODEL), lambda j, i: (i, 0)),
            pl.BlockSpec((D_MODEL, tn), lambda j, i: (0, j)),
        ],
        out_specs=pl.BlockSpec((TM_PROJ, tn), lambda j, i: (i, j)),
        out_shape=jax.ShapeDtypeStruct((T_ALL, n_cols), out_dtype),
        scratch_shapes=[pltpu.VMEM((D_MODEL, tn), BF16)],
        compiler_params=_params(2),
        name=name,
    )(h, w)


def _gate_prep(L, gates, bias):
    g = gates + bias
    lane = lax.broadcasted_iota(jnp.int32, g.shape, 1)
    logsig = jnp.minimum(g, 0.0) - jnp.log(1.0 + jnp.exp(-jnp.abs(g)))
    igf = jnp.where(lane < N_HEADS, g, logsig)
    row = lax.broadcasted_iota(jnp.int32, (L, L), 0)
    col = lax.broadcasted_iota(jnp.int32, (L, L), 1)
    tril = (row >= col).astype(BF16)
    triu = (row <= col).astype(BF16)
    eye8 = (lax.broadcasted_iota(jnp.int32, (8, LANES), 0)
            == lax.broadcasted_iota(jnp.int32, (8, LANES), 1)).astype(BF16)
    parts = _split3(igf)
    bcum = sum(_dot(tril, p) for p in parts)
    gt = sum(_dot_nt(eye8, p) for p in parts)
    bt = sum(_dot(p, triu) for p in _split3(gt))
    return igf, bcum, gt, bt


def _mlstm_head(L, h, q, k, v, zo, prep, m_prev, c_prev, n_prev, hnorm):
    igf, bcum, gt, bt = prep
    bcol = bcum[:, N_HEADS + h:N_HEADS + h + 1]
    brow = bt[N_HEADS + h:N_HEADS + h + 1, :]
    igrow = gt[h:h + 1, :]
    igcol = igf[:, h:h + 1]
    row = lax.broadcasted_iota(jnp.int32, (L, L), 0)
    col = lax.broadcasted_iota(jnp.int32, (L, L), 1)
    dmat = jnp.where(row >= col, bcol - brow + igrow, -jnp.inf)
    log_inter = bcol + m_prev
    m_tok = jnp.maximum(log_inter, jnp.max(dmat, axis=-1, keepdims=True))
    w_inter = jnp.exp(log_inter - m_tok)
    scale = D_QK ** -0.5
    qb = q.astype(BF16)
    kb = k.astype(BF16)
    vb = v.astype(BF16)
    qf = q.astype(F32)
    kf = k.astype(F32)
    s = jnp.exp(dmat - m_tok) * (_dot_nt(qb, kb) * scale)
    num = (w_inter * scale) * _dot(qb, c_prev.astype(BF16)) + _dot(s.astype(BF16), vb)
    qn = jnp.sum(qf * n_prev, axis=-1, keepdims=True) * scale
    den = w_inter * qn + jnp.sum(s, axis=-1, keepdims=True)
    hout = num / jnp.maximum(jnp.abs(den), jnp.exp(-m_tok))
    b_last = bcol[L - 1:L, :]
    log_kv = b_last - bcol + igcol
    m_new = jnp.maximum(b_last + m_prev, jnp.max(log_kv, axis=0, keepdims=True))
    f_st = jnp.exp(b_last + m_prev - m_new)
    kw = kf * jnp.exp(log_kv - m_new)
    c_new = f_st * c_prev + _dot_tn(kw.astype(BF16), vb)
    n_new = f_st * n_prev + jnp.sum(kw, axis=0, keepdims=True)
    hn = hout * lax.rsqrt(jnp.mean(hout * hout, axis=-1, keepdims=True) + EPS) * hnorm
    hg = jax.nn.sigmoid(zo.astype(F32)) * hn
    return hg, c_new, n_new, m_new


def _mlstm_prompt_kernel(q_ref, k_ref, v0_ref, v1_ref, zo0_ref, zo1_ref, g_ref, bias_ref, hn_ref,
                         hg_ref, c_ref, n_ref, m_ref):
    L = CHUNK

    @pl.when(pl.program_id(1) == 0)
    def _():
        c_ref[...] = jnp.zeros_like(c_ref)
        n_ref[...] = jnp.zeros_like(n_ref)
        m_ref[...] = jnp.zeros_like(m_ref)

    prep = _gate_prep(L, g_ref[...], bias_ref[...])
    mrow = m_ref[0]
    lane = lax.broadcasted_iota(jnp.int32, mrow.shape, 1)
    v_refs = (v0_ref, v1_ref)
    zo_refs = (zo0_ref, zo1_ref)
    for h in range(N_HEADS):
        hv = (h % 2) * D_V
        hg, c_new, n_new, m_new = _mlstm_head(
            L, h,
            q_ref[:, h * D_QK:(h + 1) * D_QK], k_ref[:, h * D_QK:(h + 1) * D_QK],
            v_refs[h // 2][:, hv:hv + D_V], zo_refs[h // 2][:, hv:hv + D_V],
            prep, mrow[:, h:h + 1], c_ref[0, h], n_ref[0, h:h + 1, :],
            hn_ref[:, h * D_V:(h + 1) * D_V])
        hg_ref[:, h * D_V:(h + 1) * D_V] = hg.astype(hg_ref.dtype)
        c_ref[0, h] = c_new
        n_ref[0, h:h + 1, :] = n_new
        mrow = jnp.where(lane == h, m_new, mrow)
    m_ref[0] = mrow


def _mlstm_prompt_call(z_main, z_if, bias, hnorm):
    nc = SEQ // CHUNK
    blk = lambda w, c: pl.BlockSpec((CHUNK, w), lambda b, t, c=c: (b * nc + t, c))
    return pl.pallas_call(
        _mlstm_prompt_kernel,
        grid=(BATCH, nc),
        in_specs=[
            blk(D_QKB, 3), blk(D_QKB, 4), blk(D_QKB, 5), blk(D_QKB, 6), blk(D_QKB, 7),
            blk(D_QKB, 8), blk(LANES, 0),
            pl.BlockSpec((1, LANES), lambda b, t: (0, 0)),
            pl.BlockSpec((1, D_VB), lambda b, t: (0, 0)),
        ],
        out_specs=[
            pl.BlockSpec((CHUNK, D_VB), lambda b, t: (b * nc + t, 0)),
            pl.BlockSpec((1, N_HEADS, D_QK, D_V), lambda b, t: (b, 0, 0, 0)),
            pl.BlockSpec((1, N_HEADS, D_QK), lambda b, t: (b, 0, 0)),
            pl.BlockSpec((1, 1, LANES), lambda b, t: (b, 0, 0)),
        ],
        out_shape=[
            jax.ShapeDtypeStruct((T_P, D_VB), BF16),
            jax.ShapeDtypeStruct((BATCH, N_HEADS, D_QK, D_V), F32),
            jax.ShapeDtypeStruct((BATCH, N_HEADS, D_QK), F32),
            jax.ShapeDtypeStruct((BATCH, 1, LANES), F32),
        ],
        compiler_params=_params(2),
        name="mlstm_prompt",
    )(z_main, z_main, z_main, z_main, z_main, z_main, z_if, bias, hnorm)


def _mlstm_sample_kernel(q_ref, k_ref, v0_ref, v1_ref, zo0_ref, zo1_ref, g_ref, bias_ref, hn_ref,
                         c0_ref, n0_ref, m0_ref, hg_ref, c_ref, n_ref, m_ref):
    L = DEC_SEQ
    q = q_ref[...].astype(F32)
    k = k_ref[...].astype(F32)
    vs = (v0_ref[...].astype(F32), v1_ref[...].astype(F32))
    zos = (zo0_ref[...].astype(F32), zo1_ref[...].astype(F32))
    gates = g_ref[...]
    for s in range(SAMPLE_SEQS_PER_STEP):
        rows = slice(s * L, (s + 1) * L)
        prep = _gate_prep(L, gates[rows], bias_ref[...])
        mrow = m0_ref[s]
        lane = lax.broadcasted_iota(jnp.int32, mrow.shape, 1)
        for h in range(N_HEADS):
            hv = (h % 2) * D_V
            hg, c_new, n_new, m_new = _mlstm_head(
                L, h,
                q[rows, h * D_QK:(h + 1) * D_QK], k[rows, h * D_QK:(h + 1) * D_QK],
                vs[h // 2][rows, hv:hv + D_V], zos[h // 2][rows, hv:hv + D_V],
                prep, mrow[:, h:h + 1], c0_ref[s, h], n0_ref[s, h:h + 1, :],
                hn_ref[:, h * D_V:(h + 1) * D_V])
            hg_ref[rows, h * D_V:(h + 1) * D_V] = hg
            c_ref[s, h] = c_new
            n_ref[s, h:h + 1, :] = n_new
            mrow = jnp.where(lane == h, m_new, mrow)
        m_ref[s] = mrow


def _mlstm_sample_call(z_main, z_if, bias, hnorm, c0, n0, m0):
    nb = SAMPLE_SEQS_PER_STEP
    rows = nb * DEC_SEQ
    base = T_P // rows
    blk = lambda w, c: pl.BlockSpec((rows, w), lambda g, c=c: (base + g, c))
    return pl.pallas_call(
        _mlstm_sample_kernel,
        grid=(DEC_BATCH // nb,),
        in_specs=[
            blk(D_QKB, 3), blk(D_QKB, 4), blk(D_QKB, 5), blk(D_QKB, 6), blk(D_QKB, 7),
            blk(D_QKB, 8), blk(LANES, 0),
            pl.BlockSpec((1, LANES), lambda g: (0, 0)),
            pl.BlockSpec((1, D_VB), lambda g: (0, 0)),
            pl.BlockSpec((nb, N_HEADS, D_QK, D_V), lambda g: (g, 0, 0, 0)),
            pl.BlockSpec((nb, N_HEADS, D_QK), lambda g: (g, 0, 0)),
            pl.BlockSpec((nb, 1, LANES), lambda g: (g, 0, 0)),
        ],
        out_specs=[
            pl.BlockSpec((rows, D_VB), lambda g: (g, 0)),
            pl.BlockSpec((nb, N_HEADS, D_QK, D_V), lambda g: (g, 0, 0, 0)),
            pl.BlockSpec((nb, N_HEADS, D_QK), lambda g: (g, 0, 0)),
            pl.BlockSpec((nb, 1, LANES), lambda g: (g, 0, 0)),
        ],
        out_shape=[
            jax.ShapeDtypeStruct((T_S, D_VB), F32),
            jax.ShapeDtypeStruct((DEC_BATCH, N_HEADS, D_QK, D_V), F32),
            jax.ShapeDtypeStruct((DEC_BATCH, N_HEADS, D_QK), F32),
            jax.ShapeDtypeStruct((DEC_BATCH, 1, LANES), F32),
        ],
        compiler_params=_params(1),
        name="mlstm_sample",
    )(z_main, z_main, z_main, z_main, z_main, z_main, z_if, bias, hnorm, c0, n0, m0)


NP_MIX = T_P // TM_MIX
TILES_PER_SEQ = SEQ // TM_MIX


def _mixer_kernel(cb_ref, cc_ref, cx_ref, ccp_ref, cxp_ref, buf_ref, hgp_ref, hgs_ref, ga_ref,
                  gb_ref, cw_ref, wa_ref, wb_ref, mix_ref, utail_ref, us_ref, a_ref):
    i = pl.program_id(0)
    u = cc_ref[...].astype(F32) * cx_ref[...].astype(F32)
    w0 = cw_ref[0:1, :]
    w1 = cw_ref[1:2, :]
    w2 = cw_ref[2:3, :]
    rowid = lax.broadcasted_iota(jnp.int32, u.shape, 0)
    um1 = pltpu.roll(u, 1, 0)
    um2 = pltpu.roll(u, 2, 0)
    cb = cb_ref[...].astype(F32)

    @pl.when(i < NP_MIX)
    def _():
        valid = (i % TILES_PER_SEQ != 0).astype(F32)
        uprev = ccp_ref[...].astype(F32) * cxp_ref[...].astype(F32) * valid
        p1 = uprev[15:16, :]
        p2 = uprev[14:15, :]
        m1 = jnp.where(rowid == 0, p1, um1)
        m2 = jnp.where(rowid == 0, p2, jnp.where(rowid == 1, p1, um2))
        a_ref[...] = (cb * (w0 * m2 + w1 * m1 + w2 * u)).astype(BF16)

        @pl.when(i % TILES_PER_SEQ == TILES_PER_SEQ - 1)
        def _():
            utail_ref[...] = u[TM_MIX - 8:, :]

    @pl.when(i >= NP_MIX)
    def _():
        pos = rowid % DEC_SEQ
        bp = buf_ref[...]
        m1 = jnp.where(pos >= 1, um1, pltpu.roll(bp, TM_MIX - 1, 0))
        m2 = jnp.where(pos >= 2, um2, bp)
        a_ref[...] = (cb * (w0 * m2 + w1 * m1 + w2 * u)).astype(BF16)
        us_ref[...] = u

    ya = _dot(a_ref[...], wa_ref[...])

    def finish(hg):
        yb = _dot(hg, wb_ref[...])
        mix = (jax.nn.sigmoid(ga_ref[...].astype(F32)) * ya
               + jax.nn.sigmoid(gb_ref[...].astype(F32)) * yb)
        mix_ref[...] = mix.astype(mix_ref.dtype)

    @pl.when(i < NP_MIX)
    def _():
        finish(hgp_ref[...])

    @pl.when(i >= NP_MIX)
    def _():
        finish(hgs_ref[...].astype(BF16))


def _mixer_call(z_main, z_gab, bufpad, hg_p, hg_s, conv_w, wa, wb):
    tm = TM_MIX
    pidx = lambda i: jnp.minimum(i, NP_MIX - 1)
    sidx = lambda i: jnp.maximum(i - NP_MIX, 0)
    return pl.pallas_call(
        _mixer_kernel,
        grid=(T_ALL // tm,),
        in_specs=[
            pl.BlockSpec((tm, D_CONV), lambda i: (i, 0)),
            pl.BlockSpec((tm, D_CONV), lambda i: (i, 1)),
            pl.BlockSpec((tm, D_CONV), lambda i: (i, 2)),
            pl.BlockSpec((16, D_CONV), lambda i: (jnp.maximum(i * (tm // 16) - 1, 0), 1)),
            pl.BlockSpec((16, D_CONV), lambda i: (jnp.maximum(i * (tm // 16) - 1, 0), 2)),
            pl.BlockSpec((tm, D_CONV), lambda i: (sidx(i), 0)),
            pl.BlockSpec((tm, D_VB), lambda i: (pidx(i), 0)),
            pl.BlockSpec((tm, D_VB), lambda i: (sidx(i), 0)),
            pl.BlockSpec((tm, D_MODEL), lambda i: (i, 0)),
            pl.BlockSpec((tm, D_MODEL), lambda i: (i, 1)),
            pl.BlockSpec((3, D_CONV), lambda i: (0, 0)),
            pl.BlockSpec((D_CONV, D_MODEL), lambda i: (0, 0)),
            pl.BlockSpec((D_VB, D_MODEL), lambda i: (0, 0)),
        ],
        out_specs=[
            pl.BlockSpec((tm, D_MODEL), lambda i: (i, 0)),
            pl.BlockSpec((8, D_CONV), lambda i: (jnp.minimum(i // TILES_PER_SEQ, BATCH - 1), 0)),
            pl.BlockSpec((tm, D_CONV), lambda i: (sidx(i), 0)),
        ],
        out_shape=[
            jax.ShapeDtypeStruct((T_ALL, D_MODEL), BF16),
            jax.ShapeDtypeStruct((BATCH * 8, D_CONV), F32),
            jax.ShapeDtypeStruct((T_S, D_CONV), F32),
        ],
        scratch_shapes=[pltpu.VMEM((tm, D_CONV), BF16)],
        compiler_params=_params(1),
        name="mixer",
    )(z_main, z_main, z_main, z_main, z_main, bufpad, hg_p, hg_s, z_gab, z_gab, conv_w, wa, wb)


def _router_kernel(mix_ref, xp_ref, xs_ref, wo_ref, g2_ref, wr_hi_ref, wr_lo_ref,
                   x2_ref, hn2_ref, rt_ref):
    i = pl.program_id(0)

    def emit(x):
        x2 = x + _dot(mix_ref[...], wo_ref[...])
        x2_ref[...] = x2
        hn2 = x2 * lax.rsqrt(jnp.mean(x2 * x2, axis=-1, keepdims=True) + EPS) * g2_ref[...]
        hn2_ref[...] = hn2
        hi = hn2.astype(BF16)
        lo = (hn2 - hi.astype(F32)).astype(BF16)
        logits = _dot(hi, wr_hi_ref[...]) + (_dot(lo, wr_hi_ref[...]) + _dot(hi, wr_lo_ref[...]))
        lane = lax.broadcasted_iota(jnp.int32, logits.shape, 1)
        lanef = lane.astype(F32)
        neg = -jnp.inf
        gl = jnp.where(lane < N_GROUPS, logits, neg)
        gmax = jnp.max(gl, axis=-1, keepdims=True)
        g_star = jnp.min(jnp.where(gl == gmax, lanef, float(LANES)), axis=-1, keepdims=True)
        p_grp = 1.0 / jnp.sum(jnp.exp(gl - gmax), axis=-1, keepdims=True)
        lo_lane = N_GROUPS + EXPERTS_PER_GROUP * g_star
        el = jnp.where((lanef >= lo_lane) & (lanef < lo_lane + EXPERTS_PER_GROUP), logits, neg)
        e1 = jnp.max(el, axis=-1, keepdims=True)
        i1 = jnp.min(jnp.where(el == e1, lanef, float(LANES)), axis=-1, keepdims=True)
        el2 = jnp.where(lanef == i1, neg, el)
        e2 = jnp.max(el2, axis=-1, keepdims=True)
        i2 = jnp.min(jnp.where(el2 == e2, lanef, float(LANES)), axis=-1, keepdims=True)
        t = jnp.exp(e2 - e1)
        v1 = p_grp / (1.0 + t)
        v2 = v1 * t
        rt = jnp.where(lane == 0, i1 - N_GROUPS,
             jnp.where(lane == 1, i2 - N_GROUPS,
             jnp.where(lane == 2, v1, jnp.where(lane == 3, v2, 0.0))))
        rt_ref[...] = rt

    @pl.when(i < NP_MIX)
    def _():
        emit(xp_ref[...])

    @pl.when(i >= NP_MIX)
    def _():
        emit(xs_ref[...])


def _router_call(mix, xp, xs, wo, g2, wr_hi, wr_lo):
    tm = TM_MIX
    pidx = lambda i: jnp.minimum(i, NP_MIX - 1)
    sidx = lambda i: jnp.maximum(i - NP_MIX, 0)
    return pl.pallas_call(
        _router_kernel,
        grid=(T_ALL // tm,),
        in_specs=[
            pl.BlockSpec((tm, D_MODEL), lambda i: (i, 0)),
            pl.BlockSpec((tm, D_MODEL), lambda i: (pidx(i), 0)),
            pl.BlockSpec((tm, D_MODEL), lambda i: (sidx(i), 0)),
            pl.BlockSpec((D_MODEL, D_MODEL), lambda i: (0, 0)),
            pl.BlockSpec((1, D_MODEL), lambda i: (0, 0)),
            pl.BlockSpec((D_MODEL, LANES), lambda i: (0, 0)),
            pl.BlockSpec((D_MODEL, LANES), lambda i: (0, 0)),
        ],
        out_specs=[
            pl.BlockSpec((tm, D_MODEL), lambda i: (i, 0)),
            pl.BlockSpec((tm, D_MODEL), lambda i: (i, 0)),
            pl.BlockSpec((tm, LANES), lambda i: (i, 0)),
        ],
        out_shape=[
            jax.ShapeDtypeStruct((T_ALL, D_MODEL), F32),
            jax.ShapeDtypeStruct((T_ALL, D_MODEL), F32),
            jax.ShapeDtypeStruct((T_ALL, LANES), F32),
        ],
        compiler_params=_params(1),
        name="router",
    )(mix, xp, xs, wo, g2, wr_hi, wr_lo)


def _row_copy(src_hbm, row, dst_ref, slot, r, sem):
    return pltpu.make_async_copy(
        src_hbm.at[pl.ds(row, 1), :], dst_ref.at[slot, pl.ds(r, 1), :], sem.at[slot])


def _gather_start(src_hbm, idx_ref, dst_ref, slot, sem, n_rows):
    def body(r, c):
        _row_copy(src_hbm, idx_ref[0, 0, r], dst_ref, slot, r, sem).start()
        return c
    lax.fori_loop(0, n_rows, body, 0, unroll=8)


def _gather_wait(src_hbm, dst_ref, slot, sem, n_rows):
    def body(r, c):
        _row_copy(src_hbm, 0, dst_ref, slot, r, sem).wait()
        return c
    lax.fori_loop(0, n_rows, body, 0, unroll=8)


def _experts_kernel(te_ref, na_ref, idx0_ref, idxn_ref, hn_hbm, wg_ref, wu_ref, wd_ref,
                    o_ref, xbuf, sem):
    j = pl.program_id(0)
    n_active = na_ref[0]
    slot = j % 2

    @pl.when(j == 0)
    def _():
        _gather_start(hn_hbm, idx0_ref, xbuf, 0, sem, TM_EXP)

    @pl.when(j + 1 < n_active)
    def _():
        _gather_start(hn_hbm, idxn_ref, xbuf, 1 - slot, sem, TM_EXP)

    @pl.when(j < n_active)
    def _():
        _gather_wait(hn_hbm, xbuf, slot, sem, TM_EXP)
        x = xbuf[slot].astype(BF16)
        a = _dot(x, wg_ref[0])
        u = _dot(x, wu_ref[0])
        act = (a * jax.nn.sigmoid(a) * u).astype(BF16)
        o_ref[...] = _dot(act, wd_ref[0])

    @pl.when(j >= n_active)
    def _():
        o_ref[...] = jnp.zeros_like(o_ref)


def _experts_call(tile_expert, n_active, src_tok, hn2, wg, wu, wd):
    nt = NT_EXP
    grid_spec = pltpu.PrefetchScalarGridSpec(
        num_scalar_prefetch=2,
        grid=(nt,),
        in_specs=[
            pl.BlockSpec((1, 1, TM_EXP), lambda j, te, na: (0, 0, 0), memory_space=pltpu.SMEM),
            pl.BlockSpec((1, 1, TM_EXP), lambda j, te, na: (jnp.minimum(j + 1, nt - 1), 0, 0),
                         memory_space=pltpu.SMEM),
            pl.BlockSpec(memory_space=pl.ANY),
            pl.BlockSpec((1, D_MODEL, D_EXPERT), lambda j, te, na: (te[j], 0, 0)),
            pl.BlockSpec((1, D_MODEL, D_EXPERT), lambda j, te, na: (te[j], 0, 0)),
            pl.BlockSpec((1, D_EXPERT, D_MODEL), lambda j, te, na: (te[j], 0, 0)),
        ],
        out_specs=pl.BlockSpec((TM_EXP, D_MODEL), lambda j, te, na: (j, 0)),
        scratch_shapes=[
            pltpu.VMEM((2, TM_EXP, D_MODEL), F32),
            pltpu.SemaphoreType.DMA((2,)),
        ],
    )
    return pl.pallas_call(
        _experts_kernel,
        grid_spec=grid_spec,
        out_shape=jax.ShapeDtypeStruct((R_EXP, D_MODEL), F32),
        compiler_params=_params(1),
        name="experts",
    )(tile_expert, n_active, src_tok, src_tok, hn2, wg, wu, wd)


def _combine_kernel(p1c_ref, p1n_ref, p2c_ref, p2n_ref, eo_hbm, x2_ref, rt_ref, gf_ref,
                    yp_ref, ys_ref, buf1, buf2, sem1, sem2):
    i = pl.program_id(0)
    n = pl.num_programs(0)
    slot = i % 2

    @pl.when(i == 0)
    def _():
        _gather_start(eo_hbm, p1c_ref, buf1, 0, sem1, TM_MIX)
        _gather_start(eo_hbm, p2c_ref, buf2, 0, sem2, TM_MIX)

    @pl.when(i + 1 < n)
    def _():
        _gather_start(eo_hbm, p1n_ref, buf1, 1 - slot, sem1, TM_MIX)
        _gather_start(eo_hbm, p2n_ref, buf2, 1 - slot, sem2, TM_MIX)

    _gather_wait(eo_hbm, buf1, slot, sem1, TM_MIX)
    _gather_wait(eo_hbm, buf2, slot, sem2, TM_MIX)
    rt = rt_ref[...]
    y = x2_ref[...] + rt[:, 2:3] * buf1[slot] + rt[:, 3:4] * buf2[slot]
    out = y * lax.rsqrt(jnp.mean(y * y, axis=-1, keepdims=True) + EPS) * gf_ref[...]

    @pl.when(i < NP_MIX)
    def _():
        yp_ref[...] = out

    @pl.when(i >= NP_MIX)
    def _():
        ys_ref[...] = out


def _combine_call(pos1, pos2, eo, x2, rt, gf):
    tm = TM_MIX
    n = T_ALL // tm
    cur = lambda i: (i, 0, 0)
    nxt = lambda i: (jnp.minimum(i + 1, n - 1), 0, 0)
    smem = lambda f: pl.BlockSpec((1, 1, tm), f, memory_space=pltpu.SMEM)
    return pl.pallas_call(
        _combine_kernel,
        grid=(n,),
        in_specs=[
            smem(lambda i: (0, 0, 0)), smem(nxt), smem(lambda i: (0, 0, 0)), smem(nxt),
            pl.BlockSpec(memory_space=pl.ANY),
            pl.BlockSpec((tm, D_MODEL), lambda i: (i, 0)),
            pl.BlockSpec((tm, LANES), lambda i: (i, 0)),
            pl.BlockSpec((1, D_MODEL), lambda i: (0, 0)),
        ],
        out_specs=[
            pl.BlockSpec((tm, D_MODEL), lambda i: (jnp.minimum(i, NP_MIX - 1), 0)),
            pl.BlockSpec((tm, D_MODEL), lambda i: (jnp.maximum(i - NP_MIX, 0), 0)),
        ],
        out_shape=[
            jax.ShapeDtypeStruct((T_P, D_MODEL), F32),
            jax.ShapeDtypeStruct((T_S, D_MODEL), F32),
        ],
        scratch_shapes=[
            pltpu.VMEM((2, tm, D_MODEL), F32),
            pltpu.VMEM((2, tm, D_MODEL), F32),
            pltpu.SemaphoreType.DMA((2,)),
            pltpu.SemaphoreType.DMA((2,)),
        ],
        compiler_params=_params(1),
        name="combine",
    )(pos1, pos1, pos2, pos2, eo, x2, rt, gf)


def _routing_tables(rt):
    eid = rt[:, 0:TOP_K].astype(jnp.int32).reshape(-1)
    onehot = (eid[:, None] == jnp.arange(N_EXPERTS, dtype=jnp.int32)[None, :]).astype(jnp.int32)
    csum = jnp.cumsum(onehot, axis=0)
    rank = jnp.take_along_axis(csum, eid[:, None], axis=1)[:, 0] - 1
    counts = csum[-1]
    ntile = (counts + TM_EXP - 1) // TM_EXP
    tile_end = jnp.cumsum(ntile)
    offset = (tile_end - ntile) * TM_EXP
    pos = offset[eid] + rank
    n_active = tile_end[-1]
    tiles = jnp.arange(NT_EXP, dtype=jnp.int32)
    tile_expert = jnp.sum((tiles[:, None] >= tile_end[None, :]).astype(jnp.int32), axis=1)
    last_expert = jnp.sum((n_active - 1 >= tile_end).astype(jnp.int32))
    tile_expert = jnp.minimum(tile_expert, last_expert).astype(jnp.int32)
    tok = jnp.arange(N_ASSIGN, dtype=jnp.int32) // TOP_K
    src_tok = jnp.zeros((R_EXP,), jnp.int32).at[pos].set(tok)
    pos2d = pos.reshape(T_ALL, TOP_K)
    return (tile_expert, n_active.reshape(1).astype(jnp.int32),
            src_tok.reshape(NT_EXP, 1, TM_EXP),
            pos2d[:, 0].reshape(T_ALL // TM_MIX, 1, TM_MIX),
            pos2d[:, 1].reshape(T_ALL // TM_MIX, 1, TM_MIX))


def kernel(x_prompt, x_sample, state_conv, state_C, state_n, state_m, norm1, w_in, b_if, conv_w,
           head_norm, w_a_out, w_b_out, w_out, norm2, w_grp, w_rt, w_expert_gate, w_expert_up,
           w_expert_down, norm_final):
    xp = x_prompt.reshape(T_P, D_MODEL)
    xs = x_sample.reshape(T_S, D_MODEL)
    w_in0 = w_in[0]
    w_gab = w_in0[:, N_MAIN + N_IF:]
    w_if = jnp.pad(w_in0[:, N_MAIN:N_MAIN + N_IF], ((0, 0), (0, LANES - N_IF)))
    bias = jnp.pad(b_if[0].astype(F32), (0, LANES - N_IF)).reshape(1, LANES)

    hn = _norm_call(xp, xs, norm1[0].reshape(1, D_MODEL))
    z_main = _proj_call(hn, w_in0, N_MAIN, TN_PROJ, BF16, "inproj_main")
    z_gab = _proj_call(hn, w_gab, 2 * D_MODEL, TN_PROJ, BF16, "inproj_gates")
    z_if = _proj_call(hn, w_if, LANES, LANES, F32, "inproj_if")

    hnorm = head_norm[0].astype(F32).reshape(1, D_VB)
    hg_p, c_p, n_p, m_p = _mlstm_prompt_call(z_main, z_if, bias, hnorm)
    m0 = jnp.pad(state_m[0].astype(F32), ((0, 0), (0, LANES - N_HEADS))).reshape(DEC_BATCH, 1, LANES)
    hg_s, c_s, n_s, m_s = _mlstm_sample_call(
        z_main, z_if, bias, hnorm, state_C[0].astype(F32), state_n[0].astype(F32), m0)

    bufpad = jnp.pad(state_conv[0].astype(F32), ((0, 0), (0, DEC_SEQ - 2), (0, 0))).reshape(T_S, D_CONV)
    mix, utail, us = _mixer_call(z_main, z_gab, bufpad, hg_p, hg_s, conv_w[0],
                                 w_a_out[0].astype(BF16), w_b_out[0].astype(BF16))

    wr = jnp.pad(jnp.concatenate([w_grp[0], w_rt[0]], axis=1),
                 ((0, 0), (0, LANES - N_GROUPS - N_EXPERTS)))
    wr_hi = wr.astype(BF16)
    wr_lo = (wr - wr_hi.astype(F32)).astype(BF16)
    x2, hn2, rt = _router_call(mix, xp, xs, w_out[0].astype(BF16), norm2[0].reshape(1, D_MODEL),
                               wr_hi, wr_lo)

    tile_expert, n_active, src_tok, pos1, pos2 = _routing_tables(rt)
    eo = _experts_call(tile_expert, n_active, src_tok, hn2, w_expert_gate[0].astype(BF16),
                       w_expert_up[0].astype(BF16), w_expert_down[0].astype(BF16))
    y_p, y_s = _combine_call(pos1, pos2, eo, x2, rt, norm_final.reshape(1, D_MODEL))

    new_conv_p = utail.reshape(BATCH, 8, D_CONV)[:, 6:8, :][None]
    new_conv_s = us.reshape(DEC_BATCH, DEC_SEQ, D_CONV)[:, DEC_SEQ - 2:, :][None]
    return (y_p.reshape(BATCH, SEQ, D_MODEL), y_s.reshape(DEC_BATCH, DEC_SEQ, D_MODEL),
            new_conv_p, c_p[None], n_p[None], m_p[:, 0, :N_HEADS][None],
            new_conv_s, c_s[None], n_s[None], m_s[:, 0, :N_HEADS][None])
```

```python
import functools

import jax
import jax.numpy as jnp
from jax import lax
from jax.experimental import pallas as pl
from jax.experimental.pallas import tpu as pltpu

F32 = jnp.float32
BF16 = jnp.bfloat16

D_MODEL = 2048
BATCH = 4
SEQ = 2048
DEC_BATCH = 128
DEC_SEQ = 8
D_CONV = 1024
N_HEADS = 4
D_QK = 256
D_V = 512
D_QKB = N_HEADS * D_QK
D_VB = N_HEADS * D_V
CHUNK = 128
N_GROUPS = 4
EXPERTS_PER_GROUP = 4
N_EXPERTS = 16
TOP_K = 2
D_EXPERT = 1024
EPS = 1e-6

T_P = BATCH * SEQ
T_S = DEC_BATCH * DEC_SEQ
T_ALL = T_P + T_S
N_MAIN = 3 * D_CONV + 2 * D_QKB + 2 * D_VB
N_IF = 2 * N_HEADS
LANES = 128
VMEM_LIMIT = 56 * 1024 * 1024

TM_NORM = 512
TM_PROJ = 1024
TN_PROJ = 1024
TM_MIX = 256
TM_EXP = 256
N_ASSIGN = T_ALL * TOP_K
NT_EXP = N_ASSIGN // TM_EXP + N_EXPERTS
R_EXP = NT_EXP * TM_EXP
SAMPLE_SEQS_PER_STEP = 2


def _params(n_axes):
    return pltpu.CompilerParams(
        dimension_semantics=("arbitrary",) * n_axes, vmem_limit_bytes=VMEM_LIMIT)


def _split3(x):
    x1 = x.astype(BF16)
    r1 = x - x1.astype(F32)
    x2 = r1.astype(BF16)
    x3 = (r1 - x2.astype(F32)).astype(BF16)
    return x1, x2, x3


def _dot(a, b):
    return jnp.dot(a, b, preferred_element_type=F32)


def _dot_nt(a, b):
    return lax.dot_general(a, b, (((1,), (1,)), ((), ())), preferred_element_type=F32)


def _dot_tn(a, b):
    return lax.dot_general(a, b, (((0,), (0,)), ((), ())), preferred_element_type=F32)


def _norm_kernel(xp_ref, xs_ref, g_ref, o_ref):
    i = pl.program_id(0)

    def emit(x):
        r = lax.rsqrt(jnp.mean(x * x, axis=-1, keepdims=True) + EPS)
        o_ref[...] = (x * r * g_ref[...]).astype(o_ref.dtype)

    @pl.when(i < T_P // TM_NORM)
    def _():
        emit(xp_ref[...])

    @pl.when(i >= T_P // TM_NORM)
    def _():
        emit(xs_ref[...])


def _norm_call(xp, xs, g):
    npt = T_P // TM_NORM
    return pl.pallas_call(
        _norm_kernel,
        grid=(T_ALL // TM_NORM,),
        in_specs=[
            pl.BlockSpec((TM_NORM, D_MODEL), lambda i: (jnp.minimum(i, npt - 1), 0)),
            pl.BlockSpec((TM_NORM, D_MODEL), lambda i: (jnp.maximum(i - npt, 0), 0)),
            pl.BlockSpec((1, D_MODEL), lambda i: (0, 0)),
        ],
        out_specs=pl.BlockSpec((TM_NORM, D_MODEL), lambda i: (i, 0)),
        out_shape=jax.ShapeDtypeStruct((T_ALL, D_MODEL), BF16),
        compiler_params=_params(1),
        name="norm",
    )(xp, xs, g)


def _proj_kernel(h_ref, w_ref, o_ref, wb_ref):
    @pl.when(pl.program_id(1) == 0)
    def _():
        wb_ref[...] = w_ref[...].astype(BF16)

    o_ref[...] = _dot_nt(h_ref[...], wb_ref[...]).astype(o_ref.dtype)


def _proj_call(h, wt, n_cols, tn, out_dtype, name):
    return pl.pallas_call(
        _proj_kernel,
        grid=(n_cols // tn, T_ALL // TM_PROJ),
        in_specs=[
            pl.BlockSpec((TM_PROJ, D_MODEL), lambda j, i: (i, 0)),
            pl.BlockSpec((tn, D_MODEL), lambda j, i: (j, 0)),
        ],
        out_specs=pl.BlockSpec((TM_PROJ, tn), lambda j, i: (i, j)),
        out_shape=jax.ShapeDtypeStruct((T_ALL, n_cols), out_dtype),
        scratch_shapes=[pltpu.VMEM((tn, D_MODEL), BF16)],
        compiler_params=_params(2),
        name=name,
    )(h, wt)


def _gate_prep(L, gates, bias):
    g = gates + bias
    lane = lax.broadcasted_iota(jnp.int32, g.shape, 1)
    logsig = jnp.minimum(g, 0.0) - jnp.log(1.0 + jnp.exp(-jnp.abs(g)))
    igf = jnp.where(lane < N_HEADS, g, logsig)
    row = lax.broadcasted_iota(jnp.int32, (L, L), 0)
    col = lax.broadcasted_iota(jnp.int32, (L, L), 1)
    tril = (row >= col).astype(BF16)
    triu = (row <= col).astype(BF16)
    eye8 = (lax.broadcasted_iota(jnp.int32, (8, LANES), 0)
            == lax.broadcasted_iota(jnp.int32, (8, LANES), 1)).astype(BF16)
    parts = _split3(igf)
    bcum = sum(_dot(tril, p) for p in parts)
    gt = sum(_dot_nt(eye8, p) for p in parts)
    bt = sum(_dot(p, triu) for p in _split3(gt))
    return igf, bcum, gt, bt


def _mlstm_head(L, h, q, k, v, zo, prep, m_prev, c_prev, n_prev, hnorm):
    igf, bcum, gt, bt = prep
    bcol = bcum[:, N_HEADS + h:N_HEADS + h + 1]
    brow = bt[N_HEADS + h:N_HEADS + h + 1, :]
    igrow = gt[h:h + 1, :]
    igcol = igf[:, h:h + 1]
    row = lax.broadcasted_iota(jnp.int32, (L, L), 0)
    col = lax.broadcasted_iota(jnp.int32, (L, L), 1)
    dmat = jnp.where(row >= col, bcol - brow + igrow, -jnp.inf)
    log_inter = bcol + m_prev
    m_tok = jnp.maximum(log_inter, jnp.max(dmat, axis=-1, keepdims=True))
    w_inter = jnp.exp(log_inter - m_tok)
    scale = D_QK ** -0.5
    qb = q.astype(BF16)
    kb = k.astype(BF16)
    vb = v.astype(BF16)
    qf = q.astype(F32)
    kf = k.astype(F32)
    s = jnp.exp(dmat - m_tok) * (_dot_nt(qb, kb) * scale)
    num = (w_inter * scale) * _dot(qb, c_prev.astype(BF16)) + _dot(s.astype(BF16), vb)
    qn = jnp.sum(qf * n_prev, axis=-1, keepdims=True) * scale
    den = w_inter * qn + jnp.sum(s, axis=-1, keepdims=True)
    hout = num / jnp.maximum(jnp.abs(den), jnp.exp(-m_tok))
    b_last = bcol[L - 1:L, :]
    log_kv = b_last - bcol + igcol
    m_new = jnp.maximum(b_last + m_prev, jnp.max(log_kv, axis=0, keepdims=True))
    f_st = jnp.exp(b_last + m_prev - m_new)
    kw = kf * jnp.exp(log_kv - m_new)
    c_new = f_st * c_prev + _dot_tn(kw.astype(BF16), vb)
    n_new = f_st * n_prev + jnp.sum(kw, axis=0, keepdims=True)
    hn = hout * lax.rsqrt(jnp.mean(hout * hout, axis=-1, keepdims=True) + EPS) * hnorm
    hg = jax.nn.sigmoid(zo.astype(F32)) * hn
    return hg, c_new, n_new, m_new


def _mlstm_prompt_kernel(q_ref, k_ref, v0_ref, v1_ref, zo0_ref, zo1_ref, g_ref, bias_ref, hn_ref,
                         hg_ref, c_ref, n_ref, m_ref):
    L = CHUNK

    @pl.when(pl.program_id(1) == 0)
    def _():
        c_ref[...] = jnp.zeros_like(c_ref)
        n_ref[...] = jnp.zeros_like(n_ref)
        m_ref[...] = jnp.zeros_like(m_ref)

    prep = _gate_prep(L, g_ref[...], bias_ref[...])
    mrow = m_ref[0]
    lane = lax.broadcasted_iota(jnp.int32, mrow.shape, 1)
    v_refs = (v0_ref, v1_ref)
    zo_refs = (zo0_ref, zo1_ref)
    for h in range(N_HEADS):
        hv = (h % 2) * D_V
        hg, c_new, n_new, m_new = _mlstm_head(
            L, h,
            q_ref[:, h * D_QK:(h + 1) * D_QK], k_ref[:, h * D_QK:(h + 1) * D_QK],
            v_refs[h // 2][:, hv:hv + D_V], zo_refs[h // 2][:, hv:hv + D_V],
            prep, mrow[:, h:h + 1], c_ref[0, h], n_ref[0, h:h + 1, :],
            hn_ref[:, h * D_V:(h + 1) * D_V])
        hg_ref[:, h * D_V:(h + 1) * D_V] = hg.astype(hg_ref.dtype)
        c_ref[0, h] = c_new
        n_ref[0, h:h + 1, :] = n_new
        mrow = jnp.where(lane == h, m_new, mrow)
    m_ref[0] = mrow


def _mlstm_prompt_call(z_main, z_if, bias, hnorm):
    nc = SEQ // CHUNK
    blk = lambda w, c: pl.BlockSpec((CHUNK, w), lambda b, t, c=c: (b * nc + t, c))
    return pl.pallas_call(
        _mlstm_prompt_kernel,
        grid=(BATCH, nc),
        in_specs=[
            blk(D_QKB, 3), blk(D_QKB, 4), blk(D_QKB, 5), blk(D_QKB, 6), blk(D_QKB, 7),
            blk(D_QKB, 8), blk(LANES, 0),
            pl.BlockSpec((1, LANES), lambda b, t: (0, 0)),
            pl.BlockSpec((1, D_VB), lambda b, t: (0, 0)),
        ],
        out_specs=[
            pl.BlockSpec((CHUNK, D_VB), lambda b, t: (b * nc + t, 0)),
            pl.BlockSpec((1, N_HEADS, D_QK, D_V), lambda b, t: (b, 0, 0, 0)),
            pl.BlockSpec((1, N_HEADS, D_QK), lambda b, t: (b, 0, 0)),
            pl.BlockSpec((1, 1, LANES), lambda b, t: (b, 0, 0)),
        ],
        out_shape=[
            jax.ShapeDtypeStruct((T_P, D_VB), BF16),
            jax.ShapeDtypeStruct((BATCH, N_HEADS, D_QK, D_V), F32),
            jax.ShapeDtypeStruct((BATCH, N_HEADS, D_QK), F32),
            jax.ShapeDtypeStruct((BATCH, 1, LANES), F32),
        ],
        compiler_params=_params(2),
        name="mlstm_prompt",
    )(z_main, z_main, z_main, z_main, z_main, z_main, z_if, bias, hnorm)


def _mlstm_sample_kernel(q_ref, k_ref, v0_ref, v1_ref, zo0_ref, zo1_ref, g_ref, bias_ref, hn_ref,
                         c0_ref, n0_ref, m0_ref, hg_ref, c_ref, n_ref, m_ref):
    L = DEC_SEQ
    q = q_ref[...].astype(F32)
    k = k_ref[...].astype(F32)
    vs = (v0_ref[...].astype(F32), v1_ref[...].astype(F32))
    zos = (zo0_ref[...].astype(F32), zo1_ref[...].astype(F32))
    gates = g_ref[...]
    for s in range(SAMPLE_SEQS_PER_STEP):
        rows = slice(s * L, (s + 1) * L)
        prep = _gate_prep(L, gates[rows], bias_ref[...])
        mrow = m0_ref[s]
        lane = lax.broadcasted_iota(jnp.int32, mrow.shape, 1)
        for h in range(N_HEADS):
            hv = (h % 2) * D_V
            hg, c_new, n_new, m_new = _mlstm_head(
                L, h,
                q[rows, h * D_QK:(h + 1) * D_QK], k[rows, h * D_QK:(h + 1) * D_QK],
                vs[h // 2][rows, hv:hv + D_V], zos[h // 2][rows, hv:hv + D_V],
                prep, mrow[:, h:h + 1], c0_ref[s, h], n0_ref[s, h:h + 1, :],
                hn_ref[:, h * D_V:(h + 1) * D_V])
            hg_ref[rows, h * D_V:(h + 1) * D_V] = hg
            c_ref[s, h] = c_new
            n_ref[s, h:h + 1, :] = n_new
            mrow = jnp.where(lane == h, m_new, mrow)
        m_ref[s] = mrow


def _mlstm_sample_call(z_main, z_if, bias, hnorm, c0, n0, m0):
    nb = SAMPLE_SEQS_PER_STEP
    rows = nb * DEC_SEQ
    base = T_P // rows
    blk = lambda w, c: pl.BlockSpec((rows, w), lambda g, c=c: (base + g, c))
    return pl.pallas_call(
        _mlstm_sample_kernel,
        grid=(DEC_BATCH // nb,),
        in_specs=[
            blk(D_QKB, 3), blk(D_QKB, 4), blk(D_QKB, 5), blk(D_QKB, 6), blk(D_QKB, 7),
            blk(D_QKB, 8), blk(LANES, 0),
            pl.BlockSpec((1, LANES), lambda g: (0, 0)),
            pl.BlockSpec((1, D_VB), lambda g: (0, 0)),
            pl.BlockSpec((nb, N_HEADS, D_QK, D_V), lambda g: (g, 0, 0, 0)),
            pl.BlockSpec((nb, N_HEADS, D_QK), lambda g: (g, 0, 0)),
            pl.BlockSpec((nb, 1, LANES), lambda g: (g, 0, 0)),
        ],
        out_specs=[
            pl.BlockSpec((rows, D_VB), lambda g: (g, 0)),
            pl.BlockSpec((nb, N_HEADS, D_QK, D_V), lambda g: (g, 0, 0, 0)),
            pl.BlockSpec((nb, N_HEADS, D_QK), lambda g: (g, 0, 0)),
            pl.BlockSpec((nb, 1, LANES), lambda g: (g, 0, 0)),
        ],
        out_shape=[
            jax.ShapeDtypeStruct((T_S, D_VB), F32),
            jax.ShapeDtypeStruct((DEC_BATCH, N_HEADS, D_QK, D_V), F32),
            jax.ShapeDtypeStruct((DEC_BATCH, N_HEADS, D_QK), F32),
            jax.ShapeDtypeStruct((DEC_BATCH, 1, LANES), F32),
        ],
        compiler_params=_params(1),
        name="mlstm_sample",
    )(z_main, z_main, z_main, z_main, z_main, z_main, z_if, bias, hnorm, c0, n0, m0)


NP_MIX = T_P // TM_MIX
TILES_PER_SEQ = SEQ // TM_MIX


def _mixer_kernel(cb_ref, cc_ref, cx_ref, ccp_ref, cxp_ref, buf_ref, hgp_ref, hgs_ref, ga_ref,
                  gb_ref, cw_ref, wa_ref, wb_ref, mix_ref, utail_ref, us_ref, a_ref):
    i = pl.program_id(0)
    u = cc_ref[...].astype(F32) * cx_ref[...].astype(F32)
    w0 = cw_ref[0:1, :]
    w1 = cw_ref[1:2, :]
    w2 = cw_ref[2:3, :]
    rowid = lax.broadcasted_iota(jnp.int32, u.shape, 0)
    um1 = pltpu.roll(u, 1, 0)
    um2 = pltpu.roll(u, 2, 0)
    cb = cb_ref[...].astype(F32)

    @pl.when(i < NP_MIX)
    def _():
        valid = (i % TILES_PER_SEQ != 0).astype(F32)
        uprev = ccp_ref[...].astype(F32) * cxp_ref[...].astype(F32) * valid
        p1 = uprev[15:16, :]
        p2 = uprev[14:15, :]
        m1 = jnp.where(rowid == 0, p1, um1)
        m2 = jnp.where(rowid == 0, p2, jnp.where(rowid == 1, p1, um2))
        a_ref[...] = (cb * (w0 * m2 + w1 * m1 + w2 * u)).astype(BF16)

        @pl.when(i % TILES_PER_SEQ == TILES_PER_SEQ - 1)
        def _():
            utail_ref[...] = u[TM_MIX - 8:, :]

    @pl.when(i >= NP_MIX)
    def _():
        pos = rowid % DEC_SEQ
        bp = buf_ref[...]
        m1 = jnp.where(pos >= 1, um1, pltpu.roll(bp, TM_MIX - 1, 0))
        m2 = jnp.where(pos >= 2, um2, bp)
        a_ref[...] = (cb * (w0 * m2 + w1 * m1 + w2 * u)).astype(BF16)
        us_ref[...] = u

    ya = _dot(a_ref[...], wa_ref[...])

    def finish(hg):
        yb = _dot(hg, wb_ref[...])
        mix = (jax.nn.sigmoid(ga_ref[...].astype(F32)) * ya
               + jax.nn.sigmoid(gb_ref[...].astype(F32)) * yb)
        mix_ref[...] = mix.astype(mix_ref.dtype)

    @pl.when(i < NP_MIX)
    def _():
        finish(hgp_ref[...])

    @pl.when(i >= NP_MIX)
    def _():
        finish(hgs_ref[...].astype(BF16))


def _mixer_call(z_main, z_gab, bufpad, hg_p, hg_s, conv_w, wa, wb):
    tm = TM_MIX
    pidx = lambda i: jnp.minimum(i, NP_MIX - 1)
    sidx = lambda i: jnp.maximum(i - NP_MIX, 0)
    return pl.pallas_call(
        _mixer_kernel,
        grid=(T_ALL // tm,),
        in_specs=[
            pl.BlockSpec((tm, D_CONV), lambda i: (i, 0)),
            pl.BlockSpec((tm, D_CONV), lambda i: (i, 1)),
            pl.BlockSpec((tm, D_CONV), lambda i: (i, 2)),
            pl.BlockSpec((16, D_CONV), lambda i: (jnp.maximum(i * (tm // 16) - 1, 0), 1)),
            pl.BlockSpec((16, D_CONV), lambda i: (jnp.maximum(i * (tm // 16) - 1, 0), 2)),
            pl.BlockSpec((tm, D_CONV), lambda i: (sidx(i), 0)),
            pl.BlockSpec((tm, D_VB), lambda i: (pidx(i), 0)),
            pl.BlockSpec((tm, D_VB), lambda i: (sidx(i), 0)),
            pl.BlockSpec((tm, D_MODEL), lambda i: (i, 0)),
            pl.BlockSpec((tm, D_MODEL), lambda i: (i, 1)),
            pl.BlockSpec((3, D_CONV), lambda i: (0, 0)),
            pl.BlockSpec((D_CONV, D_MODEL), lambda i: (0, 0)),
            pl.BlockSpec((D_VB, D_MODEL), lambda i: (0, 0)),
        ],
        out_specs=[
            pl.BlockSpec((tm, D_MODEL), lambda i: (i, 0)),
            pl.BlockSpec((8, D_CONV), lambda i: (jnp.minimum(i // TILES_PER_SEQ, BATCH - 1), 0)),
            pl.BlockSpec((tm, D_CONV), lambda i: (sidx(i), 0)),
        ],
        out_shape=[
            jax.ShapeDtypeStruct((T_ALL, D_MODEL), BF16),
            jax.ShapeDtypeStruct((BATCH * 8, D_CONV), F32),
            jax.ShapeDtypeStruct((T_S, D_CONV), F32),
        ],
        scratch_shapes=[pltpu.VMEM((tm, D_CONV), BF16)],
        compiler_params=_params(1),
        name="mixer",
    )(z_main, z_main, z_main, z_main, z_main, bufpad, hg_p, hg_s, z_gab, z_gab, conv_w, wa, wb)


def _router_kernel(mix_ref, xp_ref, xs_ref, wo_ref, g2_ref, wr_hi_ref, wr_lo_ref,
                   x2_ref, hn2_ref, rt_ref):
    i = pl.program_id(0)

    def emit(x):
        x2 = x + _dot(mix_ref[...], wo_ref[...])
        x2_ref[...] = x2
        hn2 = x2 * lax.rsqrt(jnp.mean(x2 * x2, axis=-1, keepdims=True) + EPS) * g2_ref[...]
        hn2_ref[...] = hn2
        hi = hn2.astype(BF16)
        lo = (hn2 - hi.astype(F32)).astype(BF16)
        logits = _dot(hi, wr_hi_ref[...]) + (_dot(lo, wr_hi_ref[...]) + _dot(hi, wr_lo_ref[...]))
        lane = lax.broadcasted_iota(jnp.int32, logits.shape, 1)
        lanef = lane.astype(F32)
        neg = -jnp.inf
        gl = jnp.where(lane < N_GROUPS, logits, neg)
        gmax = jnp.max(gl, axis=-1, keepdims=True)
        g_star = jnp.min(jnp.where(gl == gmax, lanef, float(LANES)), axis=-1, keepdims=True)
        p_grp = 1.0 / jnp.sum(jnp.exp(gl - gmax), axis=-1, keepdims=True)
        lo_lane = N_GROUPS + EXPERTS_PER_GROUP * g_star
        el = jnp.where((lanef >= lo_lane) & (lanef < lo_lane + EXPERTS_PER_GROUP), logits, neg)
        e1 = jnp.max(el, axis=-1, keepdims=True)
        i1 = jnp.min(jnp.where(el == e1, lanef, float(LANES)), axis=-1, keepdims=True)
        el2 = jnp.where(lanef == i1, neg, el)
        e2 = jnp.max(el2, axis=-1, keepdims=True)
        i2 = jnp.min(jnp.where(el2 == e2, lanef, float(LANES)), axis=-1, keepdims=True)
        t = jnp.exp(e2 - e1)
        v1 = p_grp / (1.0 + t)
        v2 = v1 * t
        rt = jnp.where(lane == 0, i1 - N_GROUPS,
             jnp.where(lane == 1, i2 - N_GROUPS,
             jnp.where(lane == 2, v1, jnp.where(lane == 3, v2, 0.0))))
        rt_ref[...] = rt

    @pl.when(i < NP_MIX)
    def _():
        emit(xp_ref[...])

    @pl.when(i >= NP_MIX)
    def _():
        emit(xs_ref[...])


def _router_call(mix, xp, xs, wo, g2, wr_hi, wr_lo):
    tm = TM_MIX
    pidx = lambda i: jnp.minimum(i, NP_MIX - 1)
    sidx = lambda i: jnp.maximum(i - NP_MIX, 0)
    return pl.pallas_call(
        _router_kernel,
        grid=(T_ALL // tm,),
        in_specs=[
            pl.BlockSpec((tm, D_MODEL), lambda i: (i, 0)),
            pl.BlockSpec((tm, D_MODEL), lambda i: (pidx(i), 0)),
            pl.BlockSpec((tm, D_MODEL), lambda i: (sidx(i), 0)),
            pl.BlockSpec((D_MODEL, D_MODEL), lambda i: (0, 0)),
            pl.BlockSpec((1, D_MODEL), lambda i: (0, 0)),
            pl.BlockSpec((D_MODEL, LANES), lambda i: (0, 0)),
            pl.BlockSpec((D_MODEL, LANES), lambda i: (0, 0)),
        ],
        out_specs=[
            pl.BlockSpec((tm, D_MODEL), lambda i: (i, 0)),
            pl.BlockSpec((tm, D_MODEL), lambda i: (i, 0)),
            pl.BlockSpec((tm, LANES), lambda i: (i, 0)),
        ],
        out_shape=[
            jax.ShapeDtypeStruct((T_ALL, D_MODEL), F32),
            jax.ShapeDtypeStruct((T_ALL, D_MODEL), F32),
            jax.ShapeDtypeStruct((T_ALL, LANES), F32),
        ],
        compiler_params=_params(1),
        name="router",
    )(mix, xp, xs, wo, g2, wr_hi, wr_lo)


def _row_copy(src_hbm, row, dst_ref, slot, r, sem):
    return pltpu.make_async_copy(
        src_hbm.at[pl.ds(row, 1), :], dst_ref.at[slot, pl.ds(r, 1), :], sem.at[slot])


def _gather_start(src_hbm, idx_ref, dst_ref, slot, sem, n_rows):
    for r in range(n_rows):
        _row_copy(src_hbm, idx_ref[0, 0, r], dst_ref, slot, r, sem).start()


def _gather_wait(src_hbm, dst_ref, slot, sem, n_rows):
    pltpu.make_async_copy(src_hbm.at[pl.ds(0, n_rows), :], dst_ref.at[slot], sem.at[slot]).wait()


def _experts_kernel(te_ref, na_ref, idx0_ref, idxn_ref, hn_hbm, wg_ref, wu_ref, wd_ref,
                    o_ref, xbuf, xb_ref, sem):
    j = pl.program_id(0)
    n_active = na_ref[0]
    slot = j % 2

    @pl.when(j == 0)
    def _():
        _gather_start(hn_hbm, idx0_ref, xbuf, 0, sem, TM_EXP)

    @pl.when(j < n_active)
    def _():
        _gather_wait(hn_hbm, xbuf, slot, sem, TM_EXP)
        xb_ref[...] = xbuf[slot].astype(BF16)
        _gather_start(hn_hbm, idxn_ref, xbuf, 1 - slot, sem, TM_EXP)
        x = xb_ref[...]
        a = _dot(x, wg_ref[0])
        u = _dot(x, wu_ref[0])
        act = (a * jax.nn.sigmoid(a) * u).astype(BF16)
        o_ref[...] = _dot(act, wd_ref[0])

    @pl.when(j == n_active)
    def _():
        _gather_wait(hn_hbm, xbuf, slot, sem, TM_EXP)

    @pl.when(j >= n_active)
    def _():
        o_ref[...] = jnp.zeros_like(o_ref)


def _experts_call(tile_expert, n_active, src_tok, hn2, wg, wu, wd):
    nt = NT_EXP
    grid_spec = pltpu.PrefetchScalarGridSpec(
        num_scalar_prefetch=2,
        grid=(nt,),
        in_specs=[
            pl.BlockSpec((1, 1, TM_EXP), lambda j, te, na: (0, 0, 0), memory_space=pltpu.SMEM),
            pl.BlockSpec((1, 1, TM_EXP), lambda j, te, na: (jnp.minimum(j + 1, nt - 1), 0, 0),
                         memory_space=pltpu.SMEM),
            pl.BlockSpec(memory_space=pl.ANY),
            pl.BlockSpec((1, D_MODEL, D_EXPERT), lambda j, te, na: (te[j], 0, 0)),
            pl.BlockSpec((1, D_MODEL, D_EXPERT), lambda j, te, na: (te[j], 0, 0)),
            pl.BlockSpec((1, D_EXPERT, D_MODEL), lambda j, te, na: (te[j], 0, 0)),
        ],
        out_specs=pl.BlockSpec((TM_EXP, D_MODEL), lambda j, te, na: (j, 0)),
        scratch_shapes=[
            pltpu.VMEM((2, TM_EXP, D_MODEL), F32),
            pltpu.VMEM((TM_EXP, D_MODEL), BF16),
            pltpu.SemaphoreType.DMA((2,)),
        ],
    )
    return pl.pallas_call(
        _experts_kernel,
        grid_spec=grid_spec,
        out_shape=jax.ShapeDtypeStruct((R_EXP, D_MODEL), F32),
        compiler_params=_params(1),
        name="experts",
    )(tile_expert, n_active, src_tok, src_tok, hn2, wg, wu, wd)


def _combine_kernel(p1c_ref, p1n_ref, p2c_ref, p2n_ref, eo_hbm, x2_ref, rt_ref, gf_ref,
                    yp_ref, ys_ref, buf1, buf2, sem1, sem2):
    i = pl.program_id(0)
    n = pl.num_programs(0)
    slot = i % 2

    @pl.when(i == 0)
    def _():
        _gather_start(eo_hbm, p1c_ref, buf1, 0, sem1, TM_MIX)
        _gather_start(eo_hbm, p2c_ref, buf2, 0, sem2, TM_MIX)

    @pl.when(i + 1 < n)
    def _():
        _gather_start(eo_hbm, p1n_ref, buf1, 1 - slot, sem1, TM_MIX)
        _gather_start(eo_hbm, p2n_ref, buf2, 1 - slot, sem2, TM_MIX)

    _gather_wait(eo_hbm, buf1, slot, sem1, TM_MIX)
    _gather_wait(eo_hbm, buf2, slot, sem2, TM_MIX)
    rt = rt_ref[...]
    y = x2_ref[...] + rt[:, 2:3] * buf1[slot] + rt[:, 3:4] * buf2[slot]
    out = y * lax.rsqrt(jnp.mean(y * y, axis=-1, keepdims=True) + EPS) * gf_ref[...]

    @pl.when(i < NP_MIX)
    def _():
        yp_ref[...] = out

    @pl.when(i >= NP_MIX)
    def _():
        ys_ref[...] = out


def _combine_call(pos1, pos2, eo, x2, rt, gf):
    tm = TM_MIX
    n = T_ALL // tm
    cur = lambda i: (i, 0, 0)
    nxt = lambda i: (jnp.minimum(i + 1, n - 1), 0, 0)
    smem = lambda f: pl.BlockSpec((1, 1, tm), f, memory_space=pltpu.SMEM)
    return pl.pallas_call(
        _combine_kernel,
        grid=(n,),
        in_specs=[
            smem(lambda i: (0, 0, 0)), smem(nxt), smem(lambda i: (0, 0, 0)), smem(nxt),
            pl.BlockSpec(memory_space=pl.ANY),
            pl.BlockSpec((tm, D_MODEL), lambda i: (i, 0)),
            pl.BlockSpec((tm, LANES), lambda i: (i, 0)),
            pl.BlockSpec((1, D_MODEL), lambda i: (0, 0)),
        ],
        out_specs=[
            pl.BlockSpec((tm, D_MODEL), lambda i: (jnp.minimum(i, NP_MIX - 1), 0)),
            pl.BlockSpec((tm, D_MODEL), lambda i: (jnp.maximum(i - NP_MIX, 0), 0)),
        ],
        out_shape=[
            jax.ShapeDtypeStruct((T_P, D_MODEL), F32),
            jax.ShapeDtypeStruct((T_S, D_MODEL), F32),
        ],
        scratch_shapes=[
            pltpu.VMEM((2, tm, D_MODEL), F32),
            pltpu.VMEM((2, tm, D_MODEL), F32),
            pltpu.SemaphoreType.DMA((2,)),
            pltpu.SemaphoreType.DMA((2,)),
        ],
        compiler_params=_params(1),
        name="combine",
    )(pos1, pos1, pos2, pos2, eo, x2, rt, gf)


def _routing_tables(rt):
    eid = rt[:, 0:TOP_K].astype(jnp.int32).reshape(-1)
    onehot = (eid[:, None] == jnp.arange(N_EXPERTS, dtype=jnp.int32)[None, :]).astype(jnp.int32)
    csum = jnp.cumsum(onehot, axis=0)
    rank = jnp.take_along_axis(csum, eid[:, None], axis=1)[:, 0] - 1
    counts = csum[-1]
    ntile = (counts + TM_EXP - 1) // TM_EXP
    tile_end = jnp.cumsum(ntile)
    offset = (tile_end - ntile) * TM_EXP
    pos = offset[eid] + rank
    n_active = tile_end[-1]
    tiles = jnp.arange(NT_EXP, dtype=jnp.int32)
    tile_expert = jnp.sum((tiles[:, None] >= tile_end[None, :]).astype(jnp.int32), axis=1)
    last_expert = jnp.sum((n_active - 1 >= tile_end).astype(jnp.int32))
    tile_expert = jnp.minimum(tile_expert, last_expert).astype(jnp.int32)
    tok = jnp.arange(N_ASSIGN, dtype=jnp.int32) // TOP_K
    src_tok = jnp.zeros((R_EXP,), jnp.int32).at[pos].set(
        tok, unique_indices=True, mode="promise_in_bounds")
    pos2d = pos.reshape(T_ALL, TOP_K)
    return (tile_expert, n_active.reshape(1).astype(jnp.int32),
            src_tok.reshape(NT_EXP, 1, TM_EXP),
            pos2d[:, 0].reshape(T_ALL // TM_MIX, 1, TM_MIX),
            pos2d[:, 1].reshape(T_ALL // TM_MIX, 1, TM_MIX))


def kernel(x_prompt, x_sample, state_conv, state_C, state_n, state_m, norm1, w_in, b_if, conv_w,
           head_norm, w_a_out, w_b_out, w_out, norm2, w_grp, w_rt, w_expert_gate, w_expert_up,
           w_expert_down, norm_final):
    xp = x_prompt.reshape(T_P, D_MODEL)
    xs = x_sample.reshape(T_S, D_MODEL)
    w_in_t = jnp.swapaxes(w_in[0], 0, 1)
    w_gab = w_in_t[N_MAIN + N_IF:]
    w_if = jnp.pad(w_in_t[N_MAIN:N_MAIN + N_IF], ((0, LANES - N_IF), (0, 0)))
    bias = jnp.pad(b_if[0].astype(F32), (0, LANES - N_IF)).reshape(1, LANES)

    hn = _norm_call(xp, xs, norm1[0].reshape(1, D_MODEL))
    z_main = _proj_call(hn, w_in_t, N_MAIN, TN_PROJ, BF16, "inproj_main")
    z_gab = _proj_call(hn, w_gab, 2 * D_MODEL, TN_PROJ, BF16, "inproj_gates")
    z_if = _proj_call(hn, w_if, LANES, LANES, F32, "inproj_if")

    hnorm = head_norm[0].astype(F32).reshape(1, D_VB)
    hg_p, c_p, n_p, m_p = _mlstm_prompt_call(z_main, z_if, bias, hnorm)
    m0 = jnp.pad(state_m[0].astype(F32), ((0, 0), (0, LANES - N_HEADS))).reshape(DEC_BATCH, 1, LANES)
    hg_s, c_s, n_s, m_s = _mlstm_sample_call(
        z_main, z_if, bias, hnorm, state_C[0].astype(F32), state_n[0].astype(F32), m0)

    bufpad = jnp.pad(state_conv[0].astype(F32), ((0, 0), (0, DEC_SEQ - 2), (0, 0))).reshape(T_S, D_CONV)
    mix, utail, us = _mixer_call(z_main, z_gab, bufpad, hg_p, hg_s, conv_w[0],
                                 w_a_out[0].astype(BF16), w_b_out[0].astype(BF16))

    wr = jnp.pad(jnp.concatenate([w_grp[0], w_rt[0]], axis=1),
                 ((0, 0), (0, LANES - N_GROUPS - N_EXPERTS)))
    wr_hi = wr.astype(BF16)
    wr_lo = (wr - wr_hi.astype(F32)).astype(BF16)
    x2, hn2, rt = _router_call(mix, xp, xs, w_out[0].astype(BF16), norm2[0].reshape(1, D_MODEL),
                               wr_hi, wr_lo)

    tile_expert, n_active, src_tok, pos1, pos2 = _routing_tables(rt)
    eo = _experts_call(tile_expert, n_active, src_tok, hn2, w_expert_gate[0].astype(BF16),
                       w_expert_up[0].astype(BF16), w_expert_down[0].astype(BF16))
    y_p, y_s = _combine_call(pos1, pos2, eo, x2, rt, norm_final.reshape(1, D_MODEL))

    new_conv_p = utail.reshape(BATCH, 8, D_CONV)[:, 6:8, :][None]
    new_conv_s = us.reshape(DEC_BATCH, DEC_SEQ, D_CONV)[:, DEC_SEQ - 2:, :][None]
    return (y_p.reshape(BATCH, SEQ, D_MODEL), y_s.reshape(DEC_BATCH, DEC_SEQ, D_MODEL),
            new_conv_p, c_p[None], n_p[None], m_p[:, 0, :N_HEADS][None],
            new_conv_s, c_s[None], n_s[None], m_s[:, 0, :N_HEADS][None])
```

```python
import functools

import jax
import jax.numpy as jnp
from jax import lax
from jax.experimental import pallas as pl
from jax.experimental.pallas import tpu as pltpu

F32 = jnp.float32
BF16 = jnp.bfloat16

D_MODEL = 2048
BATCH = 4
SEQ = 2048
DEC_BATCH = 128
DEC_SEQ = 8
D_CONV = 1024
N_HEADS = 4
D_QK = 256
D_V = 512
D_QKB = N_HEADS * D_QK
D_VB = N_HEADS * D_V
CHUNK = 128
N_GROUPS = 4
EXPERTS_PER_GROUP = 4
N_EXPERTS = 16
TOP_K = 2
D_EXPERT = 1024
EPS = 1e-6

T_P = BATCH * SEQ
T_S = DEC_BATCH * DEC_SEQ
T_ALL = T_P + T_S
N_MAIN = 3 * D_CONV + 2 * D_QKB + 2 * D_VB
N_IF = 2 * N_HEADS
LANES = 128
VMEM_LIMIT = 56 * 1024 * 1024

TM_NORM = 512
TM_PROJ = 1024
TN_PROJ = 1024
TM_MIX = 256
TM_EXP = 256
N_ASSIGN = T_ALL * TOP_K
NT_EXP = N_ASSIGN // TM_EXP + N_EXPERTS
R_EXP = NT_EXP * TM_EXP
SAMPLE_SEQS_PER_STEP = 2


def _params(n_axes):
    return pltpu.CompilerParams(
        dimension_semantics=("arbitrary",) * n_axes, vmem_limit_bytes=VMEM_LIMIT)


def _split3(x):
    x1 = x.astype(BF16)
    r1 = x - x1.astype(F32)
    x2 = r1.astype(BF16)
    x3 = (r1 - x2.astype(F32)).astype(BF16)
    return x1, x2, x3


def _dot(a, b):
    return jnp.dot(a, b, preferred_element_type=F32)


def _dot_nt(a, b):
    return lax.dot_general(a, b, (((1,), (1,)), ((), ())), preferred_element_type=F32)


def _dot_tn(a, b):
    return lax.dot_general(a, b, (((0,), (0,)), ((), ())), preferred_element_type=F32)


def _norm_kernel(xp_ref, xs_ref, g_ref, o_ref):
    i = pl.program_id(0)

    def emit(x):
        r = lax.rsqrt(jnp.mean(x * x, axis=-1, keepdims=True) + EPS)
        o_ref[...] = (x * r * g_ref[...]).astype(o_ref.dtype)

    @pl.when(i < T_P // TM_NORM)
    def _():
        emit(xp_ref[...])

    @pl.when(i >= T_P // TM_NORM)
    def _():
        emit(xs_ref[...])


def _norm_call(xp, xs, g):
    npt = T_P // TM_NORM
    return pl.pallas_call(
        _norm_kernel,
        grid=(T_ALL // TM_NORM,),
        in_specs=[
            pl.BlockSpec((TM_NORM, D_MODEL), lambda i: (jnp.minimum(i, npt - 1), 0)),
            pl.BlockSpec((TM_NORM, D_MODEL), lambda i: (jnp.maximum(i - npt, 0), 0)),
            pl.BlockSpec((1, D_MODEL), lambda i: (0, 0)),
        ],
        out_specs=pl.BlockSpec((TM_NORM, D_MODEL), lambda i: (i, 0)),
        out_shape=jax.ShapeDtypeStruct((T_ALL, D_MODEL), BF16),
        compiler_params=_params(1),
        name="norm",
    )(xp, xs, g)


def _proj_kernel(h_ref, w_ref, o_ref, wb_ref):
    @pl.when(pl.program_id(1) == 0)
    def _():
        wb_ref[...] = w_ref[...].astype(BF16)

    o_ref[...] = _dot_nt(h_ref[...], wb_ref[...]).astype(o_ref.dtype)


def _proj_call(h, wt, n_cols, tn, out_dtype, name):
    return pl.pallas_call(
        _proj_kernel,
        grid=(n_cols // tn, T_ALL // TM_PROJ),
        in_specs=[
            pl.BlockSpec((TM_PROJ, D_MODEL), lambda j, i: (i, 0)),
            pl.BlockSpec((tn, D_MODEL), lambda j, i: (j, 0)),
        ],
        out_specs=pl.BlockSpec((TM_PROJ, tn), lambda j, i: (i, j)),
        out_shape=jax.ShapeDtypeStruct((T_ALL, n_cols), out_dtype),
        scratch_shapes=[pltpu.VMEM((tn, D_MODEL), BF16)],
        compiler_params=_params(2),
        name=name,
    )(h, wt)


def _gate_prep(L, gates, bias):
    g = gates + bias
    lane = lax.broadcasted_iota(jnp.int32, g.shape, 1)
    logsig = jnp.minimum(g, 0.0) - jnp.log(1.0 + jnp.exp(-jnp.abs(g)))
    igf = jnp.where(lane < N_HEADS, g, logsig)
    row = lax.broadcasted_iota(jnp.int32, (L, L), 0)
    col = lax.broadcasted_iota(jnp.int32, (L, L), 1)
    tril = (row >= col).astype(BF16)
    triu = (row <= col).astype(BF16)
    eye8 = (lax.broadcasted_iota(jnp.int32, (8, LANES), 0)
            == lax.broadcasted_iota(jnp.int32, (8, LANES), 1)).astype(BF16)
    parts = _split3(igf)
    bcum = sum(_dot(tril, p) for p in parts)
    gt = sum(_dot_nt(eye8, p) for p in parts)
    bt = sum(_dot(p, triu) for p in _split3(gt))
    return igf, bcum, gt, bt


def _mlstm_head(L, h, q, k, v, zo, prep, m_prev, c_prev, n_prev, hnorm):
    igf, bcum, gt, bt = prep
    bcol = bcum[:, N_HEADS + h:N_HEADS + h + 1]
    brow = bt[N_HEADS + h:N_HEADS + h + 1, :]
    igrow = gt[h:h + 1, :]
    igcol = igf[:, h:h + 1]
    row = lax.broadcasted_iota(jnp.int32, (L, L), 0)
    col = lax.broadcasted_iota(jnp.int32, (L, L), 1)
    dmat = jnp.where(row >= col, bcol - brow + igrow, -jnp.inf)
    log_inter = bcol + m_prev
    m_tok = jnp.maximum(log_inter, jnp.max(dmat, axis=-1, keepdims=True))
    w_inter = jnp.exp(log_inter - m_tok)
    scale = D_QK ** -0.5
    qb = q.astype(BF16)
    kb = k.astype(BF16)
    vb = v.astype(BF16)
    qf = q.astype(F32)
    kf = k.astype(F32)
    s = jnp.exp(dmat - m_tok) * (_dot_nt(qb, kb) * scale)
    num = (w_inter * scale) * _dot(qb, c_prev.astype(BF16)) + _dot(s.astype(BF16), vb)
    qn = jnp.sum(qf * n_prev, axis=-1, keepdims=True) * scale
    den = w_inter * qn + jnp.sum(s, axis=-1, keepdims=True)
    hout = num / jnp.maximum(jnp.abs(den), jnp.exp(-m_tok))
    b_last = bcol[L - 1:L, :]
    log_kv = b_last - bcol + igcol
    m_new = jnp.maximum(b_last + m_prev, jnp.max(log_kv, axis=0, keepdims=True))
    f_st = jnp.exp(b_last + m_prev - m_new)
    kw = kf * jnp.exp(log_kv - m_new)
    c_new = f_st * c_prev + _dot_tn(kw.astype(BF16), vb)
    n_new = f_st * n_prev + jnp.sum(kw, axis=0, keepdims=True)
    hn = hout * lax.rsqrt(jnp.mean(hout * hout, axis=-1, keepdims=True) + EPS) * hnorm
    hg = jax.nn.sigmoid(zo.astype(F32)) * hn
    return hg, c_new, n_new, m_new


def _mlstm_prompt_kernel(q_ref, k_ref, v0_ref, v1_ref, zo0_ref, zo1_ref, g_ref, bias_ref, hn_ref,
                         hg_ref, c_ref, n_ref, m_ref):
    L = CHUNK

    @pl.when(pl.program_id(1) == 0)
    def _():
        c_ref[...] = jnp.zeros_like(c_ref)
        n_ref[...] = jnp.zeros_like(n_ref)
        m_ref[...] = jnp.zeros_like(m_ref)

    prep = _gate_prep(L, g_ref[...], bias_ref[...])
    mrow = m_ref[0]
    lane = lax.broadcasted_iota(jnp.int32, mrow.shape, 1)
    v_refs = (v0_ref, v1_ref)
    zo_refs = (zo0_ref, zo1_ref)
    for h in range(N_HEADS):
        hv = (h % 2) * D_V
        hg, c_new, n_new, m_new = _mlstm_head(
            L, h,
            q_ref[:, h * D_QK:(h + 1) * D_QK], k_ref[:, h * D_QK:(h + 1) * D_QK],
            v_refs[h // 2][:, hv:hv + D_V], zo_refs[h // 2][:, hv:hv + D_V],
            prep, mrow[:, h:h + 1], c_ref[0, h], n_ref[0, h:h + 1, :],
            hn_ref[:, h * D_V:(h + 1) * D_V])
        hg_ref[:, h * D_V:(h + 1) * D_V] = hg.astype(hg_ref.dtype)
        c_ref[0, h] = c_new
        n_ref[0, h:h + 1, :] = n_new
        mrow = jnp.where(lane == h, m_new, mrow)
    m_ref[0] = mrow


def _mlstm_prompt_call(z_main, z_if, bias, hnorm):
    nc = SEQ // CHUNK
    blk = lambda w, c: pl.BlockSpec((CHUNK, w), lambda b, t, c=c: (b * nc + t, c))
    return pl.pallas_call(
        _mlstm_prompt_kernel,
        grid=(BATCH, nc),
        in_specs=[
            blk(D_QKB, 3), blk(D_QKB, 4), blk(D_QKB, 5), blk(D_QKB, 6), blk(D_QKB, 7),
            blk(D_QKB, 8), blk(LANES, 0),
            pl.BlockSpec((1, LANES), lambda b, t: (0, 0)),
            pl.BlockSpec((1, D_VB), lambda b, t: (0, 0)),
        ],
        out_specs=[
            pl.BlockSpec((CHUNK, D_VB), lambda b, t: (b * nc + t, 0)),
            pl.BlockSpec((1, N_HEADS, D_QK, D_V), lambda b, t: (b, 0, 0, 0)),
            pl.BlockSpec((1, N_HEADS, D_QK), lambda b, t: (b, 0, 0)),
            pl.BlockSpec((1, 1, LANES), lambda b, t: (b, 0, 0)),
        ],
        out_shape=[
            jax.ShapeDtypeStruct((T_P, D_VB), BF16),
            jax.ShapeDtypeStruct((BATCH, N_HEADS, D_QK, D_V), F32),
            jax.ShapeDtypeStruct((BATCH, N_HEADS, D_QK), F32),
            jax.ShapeDtypeStruct((BATCH, 1, LANES), F32),
        ],
        compiler_params=_params(2),
        name="mlstm_prompt",
    )(z_main, z_main, z_main, z_main, z_main, z_main, z_if, bias, hnorm)


def _mlstm_sample_kernel(q_ref, k_ref, v0_ref, v1_ref, zo0_ref, zo1_ref, g_ref, bias_ref, hn_ref,
                         c0_ref, n0_ref, m0_ref, hg_ref, c_ref, n_ref, m_ref):
    L = DEC_SEQ
    q = q_ref[...].astype(F32)
    k = k_ref[...].astype(F32)
    vs = (v0_ref[...].astype(F32), v1_ref[...].astype(F32))
    zos = (zo0_ref[...].astype(F32), zo1_ref[...].astype(F32))
    gates = g_ref[...]
    for s in range(SAMPLE_SEQS_PER_STEP):
        rows = slice(s * L, (s + 1) * L)
        prep = _gate_prep(L, gates[rows], bias_ref[...])
        mrow = m0_ref[s]
        lane = lax.broadcasted_iota(jnp.int32, mrow.shape, 1)
        for h in range(N_HEADS):
            hv = (h % 2) * D_V
            hg, c_new, n_new, m_new = _mlstm_head(
                L, h,
                q[rows, h * D_QK:(h + 1) * D_QK], k[rows, h * D_QK:(h + 1) * D_QK],
                vs[h // 2][rows, hv:hv + D_V], zos[h // 2][rows, hv:hv + D_V],
                prep, mrow[:, h:h + 1], c0_ref[s, h], n0_ref[s, h:h + 1, :],
                hn_ref[:, h * D_V:(h + 1) * D_V])
            hg_ref[rows, h * D_V:(h + 1) * D_V] = hg
            c_ref[s, h] = c_new
            n_ref[s, h:h + 1, :] = n_new
            mrow = jnp.where(lane == h, m_new, mrow)
        m_ref[s] = mrow


def _mlstm_sample_call(z_main, z_if, bias, hnorm, c0, n0, m0):
    nb = SAMPLE_SEQS_PER_STEP
    rows = nb * DEC_SEQ
    base = T_P // rows
    blk = lambda w, c: pl.BlockSpec((rows, w), lambda g, c=c: (base + g, c))
    return pl.pallas_call(
        _mlstm_sample_kernel,
        grid=(DEC_BATCH // nb,),
        in_specs=[
            blk(D_QKB, 3), blk(D_QKB, 4), blk(D_QKB, 5), blk(D_QKB, 6), blk(D_QKB, 7),
            blk(D_QKB, 8), blk(LANES, 0),
            pl.BlockSpec((1, LANES), lambda g: (0, 0)),
            pl.BlockSpec((1, D_VB), lambda g: (0, 0)),
            pl.BlockSpec((nb, N_HEADS, D_QK, D_V), lambda g: (g, 0, 0, 0)),
            pl.BlockSpec((nb, N_HEADS, D_QK), lambda g: (g, 0, 0)),
            pl.BlockSpec((nb, 1, LANES), lambda g: (g, 0, 0)),
        ],
        out_specs=[
            pl.BlockSpec((rows, D_VB), lambda g: (g, 0)),
            pl.BlockSpec((nb, N_HEADS, D_QK, D_V), lambda g: (g, 0, 0, 0)),
            pl.BlockSpec((nb, N_HEADS, D_QK), lambda g: (g, 0, 0)),
            pl.BlockSpec((nb, 1, LANES), lambda g: (g, 0, 0)),
        ],
        out_shape=[
            jax.ShapeDtypeStruct((T_S, D_VB), F32),
            jax.ShapeDtypeStruct((DEC_BATCH, N_HEADS, D_QK, D_V), F32),
            jax.ShapeDtypeStruct((DEC_BATCH, N_HEADS, D_QK), F32),
            jax.ShapeDtypeStruct((DEC_BATCH, 1, LANES), F32),
        ],
        compiler_params=_params(1),
        name="mlstm_sample",
    )(z_main, z_main, z_main, z_main, z_main, z_main, z_if, bias, hnorm, c0, n0, m0)


NP_MIX = T_P // TM_MIX
TILES_PER_SEQ = SEQ // TM_MIX


def _mixer_kernel(cb_ref, cc_ref, cx_ref, ccp_ref, cxp_ref, buf_ref, hgp_ref, hgs_ref, ga_ref,
                  gb_ref, cw_ref, wa_ref, wb_ref, mix_ref, utail_ref, us_ref, a_ref):
    i = pl.program_id(0)
    u = cc_ref[...].astype(F32) * cx_ref[...].astype(F32)
    w0 = cw_ref[0:1, :]
    w1 = cw_ref[1:2, :]
    w2 = cw_ref[2:3, :]
    rowid = lax.broadcasted_iota(jnp.int32, u.shape, 0)
    um1 = pltpu.roll(u, 1, 0)
    um2 = pltpu.roll(u, 2, 0)
    cb = cb_ref[...].astype(F32)

    @pl.when(i < NP_MIX)
    def _():
        valid = (i % TILES_PER_SEQ != 0).astype(F32)
        uprev = ccp_ref[...].astype(F32) * cxp_ref[...].astype(F32) * valid
        p1 = uprev[15:16, :]
        p2 = uprev[14:15, :]
        m1 = jnp.where(rowid == 0, p1, um1)
        m2 = jnp.where(rowid == 0, p2, jnp.where(rowid == 1, p1, um2))
        a_ref[...] = (cb * (w0 * m2 + w1 * m1 + w2 * u)).astype(BF16)

        @pl.when(i % TILES_PER_SEQ == TILES_PER_SEQ - 1)
        def _():
            utail_ref[...] = u[TM_MIX - 8:, :]

    @pl.when(i >= NP_MIX)
    def _():
        pos = rowid % DEC_SEQ
        bp = buf_ref[...]
        m1 = jnp.where(pos >= 1, um1, pltpu.roll(bp, TM_MIX - 1, 0))
        m2 = jnp.where(pos >= 2, um2, bp)
        a_ref[...] = (cb * (w0 * m2 + w1 * m1 + w2 * u)).astype(BF16)
        us_ref[...] = u

    ya = _dot(a_ref[...], wa_ref[...])

    def finish(hg):
        yb = _dot(hg, wb_ref[...])
        mix = (jax.nn.sigmoid(ga_ref[...].astype(F32)) * ya
               + jax.nn.sigmoid(gb_ref[...].astype(F32)) * yb)
        mix_ref[...] = mix.astype(mix_ref.dtype)

    @pl.when(i < NP_MIX)
    def _():
        finish(hgp_ref[...])

    @pl.when(i >= NP_MIX)
    def _():
        finish(hgs_ref[...].astype(BF16))


def _mixer_call(z_main, z_gab, bufpad, hg_p, hg_s, conv_w, wa, wb):
    tm = TM_MIX
    pidx = lambda i: jnp.minimum(i, NP_MIX - 1)
    sidx = lambda i: jnp.maximum(i - NP_MIX, 0)
    return pl.pallas_call(
        _mixer_kernel,
        grid=(T_ALL // tm,),
        in_specs=[
            pl.BlockSpec((tm, D_CONV), lambda i: (i, 0)),
            pl.BlockSpec((tm, D_CONV), lambda i: (i, 1)),
            pl.BlockSpec((tm, D_CONV), lambda i: (i, 2)),
            pl.BlockSpec((16, D_CONV), lambda i: (jnp.maximum(i * (tm // 16) - 1, 0), 1)),
            pl.BlockSpec((16, D_CONV), lambda i: (jnp.maximum(i * (tm // 16) - 1, 0), 2)),
            pl.BlockSpec((tm, D_CONV), lambda i: (sidx(i), 0)),
            pl.BlockSpec((tm, D_VB), lambda i: (pidx(i), 0)),
            pl.BlockSpec((tm, D_VB), lambda i: (sidx(i), 0)),
            pl.BlockSpec((tm, D_MODEL), lambda i: (i, 0)),
            pl.BlockSpec((tm, D_MODEL), lambda i: (i, 1)),
            pl.BlockSpec((3, D_CONV), lambda i: (0, 0)),
            pl.BlockSpec((D_CONV, D_MODEL), lambda i: (0, 0)),
            pl.BlockSpec((D_VB, D_MODEL), lambda i: (0, 0)),
        ],
        out_specs=[
            pl.BlockSpec((tm, D_MODEL), lambda i: (i, 0)),
            pl.BlockSpec((8, D_CONV), lambda i: (jnp.minimum(i // TILES_PER_SEQ, BATCH - 1), 0)),
            pl.BlockSpec((tm, D_CONV), lambda i: (sidx(i), 0)),
        ],
        out_shape=[
            jax.ShapeDtypeStruct((T_ALL, D_MODEL), BF16),
            jax.ShapeDtypeStruct((BATCH * 8, D_CONV), F32),
            jax.ShapeDtypeStruct((T_S, D_CONV), F32),
        ],
        scratch_shapes=[pltpu.VMEM((tm, D_CONV), BF16)],
        compiler_params=_params(1),
        name="mixer",
    )(z_main, z_main, z_main, z_main, z_main, bufpad, hg_p, hg_s, z_gab, z_gab, conv_w, wa, wb)


def _router_kernel(mix_ref, xp_ref, xs_ref, wo_ref, g2_ref, wr_hi_ref, wr_lo_ref,
                   x2_ref, hn2_ref, rt_ref):
    i = pl.program_id(0)

    def emit(x):
        x2 = x + _dot(mix_ref[...], wo_ref[...])
        x2_ref[...] = x2
        hn2 = x2 * lax.rsqrt(jnp.mean(x2 * x2, axis=-1, keepdims=True) + EPS) * g2_ref[...]
        hn2_ref[...] = hn2
        hi = hn2.astype(BF16)
        lo = (hn2 - hi.astype(F32)).astype(BF16)
        logits = _dot(hi, wr_hi_ref[...]) + (_dot(lo, wr_hi_ref[...]) + _dot(hi, wr_lo_ref[...]))
        lane = lax.broadcasted_iota(jnp.int32, logits.shape, 1)
        lanef = lane.astype(F32)
        neg = -jnp.inf
        gl = jnp.where(lane < N_GROUPS, logits, neg)
        gmax = jnp.max(gl, axis=-1, keepdims=True)
        g_star = jnp.min(jnp.where(gl == gmax, lanef, float(LANES)), axis=-1, keepdims=True)
        p_grp = 1.0 / jnp.sum(jnp.exp(gl - gmax), axis=-1, keepdims=True)
        lo_lane = N_GROUPS + EXPERTS_PER_GROUP * g_star
        el = jnp.where((lanef >= lo_lane) & (lanef < lo_lane + EXPERTS_PER_GROUP), logits, neg)
        e1 = jnp.max(el, axis=-1, keepdims=True)
        i1 = jnp.min(jnp.where(el == e1, lanef, float(LANES)), axis=-1, keepdims=True)
        el2 = jnp.where(lanef == i1, neg, el)
        e2 = jnp.max(el2, axis=-1, keepdims=True)
        i2 = jnp.min(jnp.where(el2 == e2, lanef, float(LANES)), axis=-1, keepdims=True)
        t = jnp.exp(e2 - e1)
        v1 = p_grp / (1.0 + t)
        v2 = v1 * t
        rt = jnp.where(lane == 0, i1 - N_GROUPS,
             jnp.where(lane == 1, i2 - N_GROUPS,
             jnp.where(lane == 2, v1, jnp.where(lane == 3, v2, 0.0))))
        rt_ref[...] = rt

    @pl.when(i < NP_MIX)
    def _():
        emit(xp_ref[...])

    @pl.when(i >= NP_MIX)
    def _():
        emit(xs_ref[...])


def _router_call(mix, xp, xs, wo, g2, wr_hi, wr_lo):
    tm = TM_MIX
    pidx = lambda i: jnp.minimum(i, NP_MIX - 1)
    sidx = lambda i: jnp.maximum(i - NP_MIX, 0)
    return pl.pallas_call(
        _router_kernel,
        grid=(T_ALL // tm,),
        in_specs=[
            pl.BlockSpec((tm, D_MODEL), lambda i: (i, 0)),
            pl.BlockSpec((tm, D_MODEL), lambda i: (pidx(i), 0)),
            pl.BlockSpec((tm, D_MODEL), lambda i: (sidx(i), 0)),
            pl.BlockSpec((D_MODEL, D_MODEL), lambda i: (0, 0)),
            pl.BlockSpec((1, D_MODEL), lambda i: (0, 0)),
            pl.BlockSpec((D_MODEL, LANES), lambda i: (0, 0)),
            pl.BlockSpec((D_MODEL, LANES), lambda i: (0, 0)),
        ],
        out_specs=[
            pl.BlockSpec((tm, D_MODEL), lambda i: (i, 0)),
            pl.BlockSpec((tm, D_MODEL), lambda i: (i, 0)),
            pl.BlockSpec((tm, LANES), lambda i: (i, 0)),
        ],
        out_shape=[
            jax.ShapeDtypeStruct((T_ALL, D_MODEL), F32),
            jax.ShapeDtypeStruct((T_ALL, D_MODEL), F32),
            jax.ShapeDtypeStruct((T_ALL, LANES), F32),
        ],
        compiler_params=_params(1),
        name="router",
    )(mix, xp, xs, wo, g2, wr_hi, wr_lo)


def _row_copy(src_hbm, row, dst_ref, slot, r, sem):
    return pltpu.make_async_copy(
        src_hbm.at[pl.ds(row, 1), :], dst_ref.at[slot, pl.ds(r, 1), :], sem.at[slot])


def _gather_start(src_hbm, idx_ref, dst_ref, slot, sem, n_rows):
    for r in range(n_rows):
        _row_copy(src_hbm, idx_ref[0, 0, r], dst_ref, slot, r, sem).start()


def _gather_wait(src_hbm, dst_ref, slot, sem, n_rows):
    pltpu.make_async_copy(src_hbm.at[pl.ds(0, n_rows), :], dst_ref.at[slot], sem.at[slot]).wait()


def _experts_kernel(te_ref, na_ref, idx0_ref, idxn_ref, hn_hbm, wg_ref, wu_ref, wd_ref,
                    o_ref, xbuf, xb_ref, sem):
    j = pl.program_id(0)
    n_active = na_ref[0]
    slot = j % 2

    @pl.when(j == 0)
    def _():
        _gather_start(hn_hbm, idx0_ref, xbuf, 0, sem, TM_EXP)

    @pl.when(j < n_active)
    def _():
        _gather_wait(hn_hbm, xbuf, slot, sem, TM_EXP)
        xb_ref[...] = xbuf[slot].astype(BF16)

    @pl.when(j + 1 <= n_active)
    def _():
        _gather_start(hn_hbm, idxn_ref, xbuf, 1 - slot, sem, TM_EXP)

    @pl.when(j < n_active)
    def _():
        x = xb_ref[...]
        a = _dot(x, wg_ref[0])
        u = _dot(x, wu_ref[0])
        act = (a * jax.nn.sigmoid(a) * u).astype(BF16)
        o_ref[...] = _dot(act, wd_ref[0])

    @pl.when(j == n_active)
    def _():
        _gather_wait(hn_hbm, xbuf, slot, sem, TM_EXP)

    @pl.when(j >= n_active)
    def _():
        o_ref[...] = jnp.zeros_like(o_ref)


def _experts_call(tile_expert, n_active, src_tok, hn2, wg, wu, wd):
    nt = NT_EXP
    grid_spec = pltpu.PrefetchScalarGridSpec(
        num_scalar_prefetch=2,
        grid=(nt,),
        in_specs=[
            pl.BlockSpec((1, 1, TM_EXP), lambda j, te, na: (0, 0, 0), memory_space=pltpu.SMEM),
            pl.BlockSpec((1, 1, TM_EXP), lambda j, te, na: (jnp.minimum(j + 1, nt - 1), 0, 0),
                         memory_space=pltpu.SMEM),
            pl.BlockSpec(memory_space=pl.ANY),
            pl.BlockSpec((1, D_MODEL, D_EXPERT), lambda j, te, na: (te[j], 0, 0)),
            pl.BlockSpec((1, D_MODEL, D_EXPERT), lambda j, te, na: (te[j], 0, 0)),
            pl.BlockSpec((1, D_EXPERT, D_MODEL), lambda j, te, na: (te[j], 0, 0)),
        ],
        out_specs=pl.BlockSpec((TM_EXP, D_MODEL), lambda j, te, na: (j, 0)),
        scratch_shapes=[
            pltpu.VMEM((2, TM_EXP, D_MODEL), F32),
            pltpu.VMEM((TM_EXP, D_MODEL), BF16),
            pltpu.SemaphoreType.DMA((2,)),
        ],
    )
    return pl.pallas_call(
        _experts_kernel,
        grid_spec=grid_spec,
        out_shape=jax.ShapeDtypeStruct((R_EXP, D_MODEL), F32),
        compiler_params=_params(1),
        name="experts",
    )(tile_expert, n_active, src_tok, src_tok, hn2, wg, wu, wd)


def _combine_kernel(p1c_ref, p1n_ref, p2c_ref, p2n_ref, eo_hbm, x2_ref, rt_ref, gf_ref,
                    yp_ref, ys_ref, buf1, buf2, sem1, sem2):
    i = pl.program_id(0)
    n = pl.num_programs(0)
    slot = i % 2

    @pl.when(i == 0)
    def _():
        _gather_start(eo_hbm, p1c_ref, buf1, 0, sem1, TM_MIX)
        _gather_start(eo_hbm, p2c_ref, buf2, 0, sem2, TM_MIX)

    @pl.when(i + 1 < n)
    def _():
        _gather_start(eo_hbm, p1n_ref, buf1, 1 - slot, sem1, TM_MIX)
        _gather_start(eo_hbm, p2n_ref, buf2, 1 - slot, sem2, TM_MIX)

    _gather_wait(eo_hbm, buf1, slot, sem1, TM_MIX)
    _gather_wait(eo_hbm, buf2, slot, sem2, TM_MIX)
    rt = rt_ref[...]
    y = x2_ref[...] + rt[:, 2:3] * buf1[slot] + rt[:, 3:4] * buf2[slot]
    out = y * lax.rsqrt(jnp.mean(y * y, axis=-1, keepdims=True) + EPS) * gf_ref[...]

    @pl.when(i < NP_MIX)
    def _():
        yp_ref[...] = out

    @pl.when(i >= NP_MIX)
    def _():
        ys_ref[...] = out


def _combine_call(pos1, pos2, eo, x2, rt, gf):
    tm = TM_MIX
    n = T_ALL // tm
    cur = lambda i: (i, 0, 0)
    nxt = lambda i: (jnp.minimum(i + 1, n - 1), 0, 0)
    smem = lambda f: pl.BlockSpec((1, 1, tm), f, memory_space=pltpu.SMEM)
    return pl.pallas_call(
        _combine_kernel,
        grid=(n,),
        in_specs=[
            smem(lambda i: (0, 0, 0)), smem(nxt), smem(lambda i: (0, 0, 0)), smem(nxt),
            pl.BlockSpec(memory_space=pl.ANY),
            pl.BlockSpec((tm, D_MODEL), lambda i: (i, 0)),
            pl.BlockSpec((tm, LANES), lambda i: (i, 0)),
            pl.BlockSpec((1, D_MODEL), lambda i: (0, 0)),
        ],
        out_specs=[
            pl.BlockSpec((tm, D_MODEL), lambda i: (jnp.minimum(i, NP_MIX - 1), 0)),
            pl.BlockSpec((tm, D_MODEL), lambda i: (jnp.maximum(i - NP_MIX, 0), 0)),
        ],
        out_shape=[
            jax.ShapeDtypeStruct((T_P, D_MODEL), F32),
            jax.ShapeDtypeStruct((T_S, D_MODEL), F32),
        ],
        scratch_shapes=[
            pltpu.VMEM((2, tm, D_MODEL), F32),
            pltpu.VMEM((2, tm, D_MODEL), F32),
            pltpu.SemaphoreType.DMA((2,)),
            pltpu.SemaphoreType.DMA((2,)),
        ],
        compiler_params=_params(1),
        name="combine",
    )(pos1, pos1, pos2, pos2, eo, x2, rt, gf)


def _routing_tables(rt):
    eid = rt[:, 0:TOP_K].astype(jnp.int32).reshape(-1)
    onehot = (eid[:, None] == jnp.arange(N_EXPERTS, dtype=jnp.int32)[None, :]).astype(jnp.int32)
    csum = jnp.cumsum(onehot, axis=0)
    rank = jnp.take_along_axis(csum, eid[:, None], axis=1)[:, 0] - 1
    counts = csum[-1]
    ntile = (counts + TM_EXP - 1) // TM_EXP
    tile_end = jnp.cumsum(ntile)
    offset = (tile_end - ntile) * TM_EXP
    pos = offset[eid] + rank
    n_active = tile_end[-1]
    tiles = jnp.arange(NT_EXP, dtype=jnp.int32)
    tile_expert = jnp.sum((tiles[:, None] >= tile_end[None, :]).astype(jnp.int32), axis=1)
    last_expert = jnp.sum((n_active - 1 >= tile_end).astype(jnp.int32))
    tile_expert = jnp.minimum(tile_expert, last_expert).astype(jnp.int32)
    tok = jnp.arange(N_ASSIGN, dtype=jnp.int32) // TOP_K
    src_tok = jnp.zeros((R_EXP,), jnp.int32).at[pos].set(
        tok, unique_indices=True, mode="promise_in_bounds")
    pos2d = pos.reshape(T_ALL, TOP_K)
    return (tile_expert, n_active.reshape(1).astype(jnp.int32),
            src_tok.reshape(NT_EXP, 1, TM_EXP),
            pos2d[:, 0].reshape(T_ALL // TM_MIX, 1, TM_MIX),
            pos2d[:, 1].reshape(T_ALL // TM_MIX, 1, TM_MIX))


def kernel(x_prompt, x_sample, state_conv, state_C, state_n, state_m, norm1, w_in, b_if, conv_w,
           head_norm, w_a_out, w_b_out, w_out, norm2, w_grp, w_rt, w_expert_gate, w_expert_up,
           w_expert_down, norm_final):
    xp = x_prompt.reshape(T_P, D_MODEL)
    xs = x_sample.reshape(T_S, D_MODEL)
    w_in_t = jnp.swapaxes(w_in[0], 0, 1)
    w_gab = w_in_t[N_MAIN + N_IF:]
    w_if = jnp.pad(w_in_t[N_MAIN:N_MAIN + N_IF], ((0, LANES - N_IF), (0, 0)))
    bias = jnp.pad(b_if[0].astype(F32), (0, LANES - N_IF)).reshape(1, LANES)

    hn = _norm_call(xp, xs, norm1[0].reshape(1, D_MODEL))
    z_main = _proj_call(hn, w_in_t, N_MAIN, TN_PROJ, BF16, "inproj_main")
    z_gab = _proj_call(hn, w_gab, 2 * D_MODEL, TN_PROJ, BF16, "inproj_gates")
    z_if = _proj_call(hn, w_if, LANES, LANES, F32, "inproj_if")

    hnorm = head_norm[0].astype(F32).reshape(1, D_VB)
    hg_p, c_p, n_p, m_p = _mlstm_prompt_call(z_main, z_if, bias, hnorm)
    m0 = jnp.pad(state_m[0].astype(F32), ((0, 0), (0, LANES - N_HEADS))).reshape(DEC_BATCH, 1, LANES)
    hg_s, c_s, n_s, m_s = _mlstm_sample_call(
        z_main, z_if, bias, hnorm, state_C[0].astype(F32), state_n[0].astype(F32), m0)

    bufpad = jnp.pad(state_conv[0].astype(F32), ((0, 0), (0, DEC_SEQ - 2), (0, 0))).reshape(T_S, D_CONV)
    mix, utail, us = _mixer_call(z_main, z_gab, bufpad, hg_p, hg_s, conv_w[0],
                                 w_a_out[0].astype(BF16), w_b_out[0].astype(BF16))

    wr = jnp.pad(jnp.concatenate([w_grp[0], w_rt[0]], axis=1),
                 ((0, 0), (0, LANES - N_GROUPS - N_EXPERTS)))
    wr_hi = wr.astype(BF16)
    wr_lo = (wr - wr_hi.astype(F32)).astype(BF16)
    x2, hn2, rt = _router_call(mix, xp, xs, w_out[0].astype(BF16), norm2[0].reshape(1, D_MODEL),
                               wr_hi, wr_lo)

    tile_expert, n_active, src_tok, pos1, pos2 = _routing_tables(rt)
    eo = _experts_call(tile_expert, n_active, src_tok, hn2, w_expert_gate[0].astype(BF16),
                       w_expert_up[0].astype(BF16), w_expert_down[0].astype(BF16))
    y_p, y_s = _combine_call(pos1, pos2, eo, x2, rt, norm_final.reshape(1, D_MODEL))

    new_conv_p = utail.reshape(BATCH, 8, D_CONV)[:, 6:8, :][None]
    new_conv_s = us.reshape(DEC_BATCH, DEC_SEQ, D_CONV)[:, DEC_SEQ - 2:, :][None]
    return (y_p.reshape(BATCH, SEQ, D_MODEL), y_s.reshape(DEC_BATCH, DEC_SEQ, D_MODEL),
            new_conv_p, c_p[None], n_p[None], m_p[:, 0, :N_HEADS][None],
            new_conv_s, c_s[None], n_s[None], m_s[:, 0, :N_HEADS][None])
```

```python
import functools

import jax
import jax.numpy as jnp
from jax import lax
from jax.experimental import pallas as pl
from jax.experimental.pallas import tpu as pltpu

F32 = jnp.float32
BF16 = jnp.bfloat16

D_MODEL = 2048
BATCH = 4
SEQ = 2048
DEC_BATCH = 128
DEC_SEQ = 8
D_CONV = 1024
N_HEADS = 4
D_QK = 256
D_V = 512
D_QKB = N_HEADS * D_QK
D_VB = N_HEADS * D_V
CHUNK = 128
N_GROUPS = 4
EXPERTS_PER_GROUP = 4
N_EXPERTS = 16
TOP_K = 2
D_EXPERT = 1024
EPS = 1e-6

T_P = BATCH * SEQ
T_S = DEC_BATCH * DEC_SEQ
T_ALL = T_P + T_S
N_MAIN = 3 * D_CONV + 2 * D_QKB + 2 * D_VB
N_IF = 2 * N_HEADS
LANES = 128
VMEM_LIMIT = 56 * 1024 * 1024

TM_NORM = 512
TM_PROJ = 1024
TN_PROJ = 1024
TM_MIX = 256
TM_EXP = 256
N_ASSIGN = T_ALL * TOP_K
NT_EXP = N_ASSIGN // TM_EXP + N_EXPERTS
R_EXP = NT_EXP * TM_EXP
SAMPLE_SEQS_PER_STEP = 2


def _params(n_axes):
    return pltpu.CompilerParams(
        dimension_semantics=("arbitrary",) * n_axes, vmem_limit_bytes=VMEM_LIMIT)


def _split3(x):
    x1 = x.astype(BF16)
    r1 = x - x1.astype(F32)
    x2 = r1.astype(BF16)
    x3 = (r1 - x2.astype(F32)).astype(BF16)
    return x1, x2, x3


def _dot(a, b):
    return jnp.dot(a, b, preferred_element_type=F32)


def _dot_nt(a, b):
    return lax.dot_general(a, b, (((1,), (1,)), ((), ())), preferred_element_type=F32)


def _dot_tn(a, b):
    return lax.dot_general(a, b, (((0,), (0,)), ((), ())), preferred_element_type=F32)


def _norm_kernel(xp_ref, xs_ref, g_ref, o_ref):
    i = pl.program_id(0)

    def emit(x):
        r = lax.rsqrt(jnp.mean(x * x, axis=-1, keepdims=True) + EPS)
        o_ref[...] = (x * r * g_ref[...]).astype(o_ref.dtype)

    @pl.when(i < T_P // TM_NORM)
    def _():
        emit(xp_ref[...])

    @pl.when(i >= T_P // TM_NORM)
    def _():
        emit(xs_ref[...])


def _norm_call(xp, xs, g):
    npt = T_P // TM_NORM
    return pl.pallas_call(
        _norm_kernel,
        grid=(T_ALL // TM_NORM,),
        in_specs=[
            pl.BlockSpec((TM_NORM, D_MODEL), lambda i: (jnp.minimum(i, npt - 1), 0)),
            pl.BlockSpec((TM_NORM, D_MODEL), lambda i: (jnp.maximum(i - npt, 0), 0)),
            pl.BlockSpec((1, D_MODEL), lambda i: (0, 0)),
        ],
        out_specs=pl.BlockSpec((TM_NORM, D_MODEL), lambda i: (i, 0)),
        out_shape=jax.ShapeDtypeStruct((T_ALL, D_MODEL), BF16),
        compiler_params=_params(1),
        name="norm",
    )(xp, xs, g)


def _proj_kernel(h_ref, w_ref, o_ref, wb_ref):
    @pl.when(pl.program_id(1) == 0)
    def _():
        wb_ref[...] = w_ref[...].astype(BF16)

    o_ref[...] = _dot_nt(h_ref[...], wb_ref[...]).astype(o_ref.dtype)


def _proj_call(h, wt, n_cols, tn, out_dtype, name):
    return pl.pallas_call(
        _proj_kernel,
        grid=(n_cols // tn, T_ALL // TM_PROJ),
        in_specs=[
            pl.BlockSpec((TM_PROJ, D_MODEL), lambda j, i: (i, 0)),
            pl.BlockSpec((tn, D_MODEL), lambda j, i: (j, 0)),
        ],
        out_specs=pl.BlockSpec((TM_PROJ, tn), lambda j, i: (i, j)),
        out_shape=jax.ShapeDtypeStruct((T_ALL, n_cols), out_dtype),
        scratch_shapes=[pltpu.VMEM((tn, D_MODEL), BF16)],
        compiler_params=_params(2),
        name=name,
    )(h, wt)


def _gate_prep(L, gates, bias):
    g = gates + bias
    lane = lax.broadcasted_iota(jnp.int32, g.shape, 1)
    logsig = jnp.minimum(g, 0.0) - jnp.log(1.0 + jnp.exp(-jnp.abs(g)))
    igf = jnp.where(lane < N_HEADS, g, logsig)
    row = lax.broadcasted_iota(jnp.int32, (L, L), 0)
    col = lax.broadcasted_iota(jnp.int32, (L, L), 1)
    tril = (row >= col).astype(BF16)
    triu = (row <= col).astype(BF16)
    eye8 = (lax.broadcasted_iota(jnp.int32, (8, LANES), 0)
            == lax.broadcasted_iota(jnp.int32, (8, LANES), 1)).astype(BF16)
    parts = _split3(igf)
    bcum = sum(_dot(tril, p) for p in parts)
    gt = sum(_dot_nt(eye8, p) for p in parts)
    bt = sum(_dot(p, triu) for p in _split3(gt))
    return igf, bcum, gt, bt


def _mlstm_head(L, h, q, k, v, zo, prep, m_prev, c_prev, n_prev, hnorm):
    igf, bcum, gt, bt = prep
    bcol = bcum[:, N_HEADS + h:N_HEADS + h + 1]
    brow = bt[N_HEADS + h:N_HEADS + h + 1, :]
    igrow = gt[h:h + 1, :]
    igcol = igf[:, h:h + 1]
    row = lax.broadcasted_iota(jnp.int32, (L, L), 0)
    col = lax.broadcasted_iota(jnp.int32, (L, L), 1)
    dmat = jnp.where(row >= col, bcol - brow + igrow, -jnp.inf)
    log_inter = bcol + m_prev
    m_tok = jnp.maximum(log_inter, jnp.max(dmat, axis=-1, keepdims=True))
    w_inter = jnp.exp(log_inter - m_tok)
    scale = D_QK ** -0.5
    qb = q.astype(BF16)
    kb = k.astype(BF16)
    vb = v.astype(BF16)
    qf = q.astype(F32)
    kf = k.astype(F32)
    s = jnp.exp(dmat - m_tok) * (_dot_nt(qb, kb) * scale)
    num = (w_inter * scale) * _dot(qb, c_prev.astype(BF16)) + _dot(s.astype(BF16), vb)
    qn = jnp.sum(qf * n_prev, axis=-1, keepdims=True) * scale
    den = w_inter * qn + jnp.sum(s, axis=-1, keepdims=True)
    hout = num / jnp.maximum(jnp.abs(den), jnp.exp(-m_tok))
    b_last = bcol[L - 1:L, :]
    log_kv = b_last - bcol + igcol
    m_new = jnp.maximum(b_last + m_prev, jnp.max(log_kv, axis=0, keepdims=True))
    f_st = jnp.exp(b_last + m_prev - m_new)
    kw = kf * jnp.exp(log_kv - m_new)
    c_new = f_st * c_prev + _dot_tn(kw.astype(BF16), vb)
    n_new = f_st * n_prev + jnp.sum(kw, axis=0, keepdims=True)
    hn = hout * lax.rsqrt(jnp.mean(hout * hout, axis=-1, keepdims=True) + EPS) * hnorm
    hg = jax.nn.sigmoid(zo.astype(F32)) * hn
    return hg, c_new, n_new, m_new


def _mlstm_prompt_kernel(q_ref, k_ref, v0_ref, v1_ref, zo0_ref, zo1_ref, g_ref, bias_ref, hn_ref,
                         hg_ref, c_ref, n_ref, m_ref):
    L = CHUNK

    @pl.when(pl.program_id(1) == 0)
    def _():
        c_ref[...] = jnp.zeros_like(c_ref)
        n_ref[...] = jnp.zeros_like(n_ref)
        m_ref[...] = jnp.zeros_like(m_ref)

    prep = _gate_prep(L, g_ref[...], bias_ref[...])
    mrow = m_ref[0]
    lane = lax.broadcasted_iota(jnp.int32, mrow.shape, 1)
    v_refs = (v0_ref, v1_ref)
    zo_refs = (zo0_ref, zo1_ref)
    for h in range(N_HEADS):
        hv = (h % 2) * D_V
        hg, c_new, n_new, m_new = _mlstm_head(
            L, h,
            q_ref[:, h * D_QK:(h + 1) * D_QK], k_ref[:, h * D_QK:(h + 1) * D_QK],
            v_refs[h // 2][:, hv:hv + D_V], zo_refs[h // 2][:, hv:hv + D_V],
            prep, mrow[:, h:h + 1], c_ref[0, h], n_ref[0, h:h + 1, :],
            hn_ref[:, h * D_V:(h + 1) * D_V])
        hg_ref[:, h * D_V:(h + 1) * D_V] = hg.astype(hg_ref.dtype)
        c_ref[0, h] = c_new
        n_ref[0, h:h + 1, :] = n_new
        mrow = jnp.where(lane == h, m_new, mrow)
    m_ref[0] = mrow


def _mlstm_prompt_call(z_main, z_if, bias, hnorm):
    nc = SEQ // CHUNK
    blk = lambda w, c: pl.BlockSpec((CHUNK, w), lambda b, t, c=c: (b * nc + t, c))
    return pl.pallas_call(
        _mlstm_prompt_kernel,
        grid=(BATCH, nc),
        in_specs=[
            blk(D_QKB, 3), blk(D_QKB, 4), blk(D_QKB, 5), blk(D_QKB, 6), blk(D_QKB, 7),
            blk(D_QKB, 8), blk(LANES, 0),
            pl.BlockSpec((1, LANES), lambda b, t: (0, 0)),
            pl.BlockSpec((1, D_VB), lambda b, t: (0, 0)),
        ],
        out_specs=[
            pl.BlockSpec((CHUNK, D_VB), lambda b, t: (b * nc + t, 0)),
            pl.BlockSpec((1, N_HEADS, D_QK, D_V), lambda b, t: (b, 0, 0, 0)),
            pl.BlockSpec((1, N_HEADS, D_QK), lambda b, t: (b, 0, 0)),
            pl.BlockSpec((1, 1, LANES), lambda b, t: (b, 0, 0)),
        ],
        out_shape=[
            jax.ShapeDtypeStruct((T_P, D_VB), BF16),
            jax.ShapeDtypeStruct((BATCH, N_HEADS, D_QK, D_V), F32),
            jax.ShapeDtypeStruct((BATCH, N_HEADS, D_QK), F32),
            jax.ShapeDtypeStruct((BATCH, 1, LANES), F32),
        ],
        compiler_params=_params(2),
        name="mlstm_prompt",
    )(z_main, z_main, z_main, z_main, z_main, z_main, z_if, bias, hnorm)


def _mlstm_sample_kernel(q_ref, k_ref, v0_ref, v1_ref, zo0_ref, zo1_ref, g_ref, bias_ref, hn_ref,
                         c0_ref, n0_ref, m0_ref, hg_ref, c_ref, n_ref, m_ref):
    L = DEC_SEQ
    q = q_ref[...].astype(F32)
    k = k_ref[...].astype(F32)
    vs = (v0_ref[...].astype(F32), v1_ref[...].astype(F32))
    zos = (zo0_ref[...].astype(F32), zo1_ref[...].astype(F32))
    gates = g_ref[...]
    for s in range(SAMPLE_SEQS_PER_STEP):
        rows = slice(s * L, (s + 1) * L)
        prep = _gate_prep(L, gates[rows], bias_ref[...])
        mrow = m0_ref[s]
        lane = lax.broadcasted_iota(jnp.int32, mrow.shape, 1)
        for h in range(N_HEADS):
            hv = (h % 2) * D_V
            hg, c_new, n_new, m_new = _mlstm_head(
                L, h,
                q[rows, h * D_QK:(h + 1) * D_QK], k[rows, h * D_QK:(h + 1) * D_QK],
                vs[h // 2][rows, hv:hv + D_V], zos[h // 2][rows, hv:hv + D_V],
                prep, mrow[:, h:h + 1], c0_ref[s, h], n0_ref[s, h:h + 1, :],
                hn_ref[:, h * D_V:(h + 1) * D_V])
            hg_ref[rows, h * D_V:(h + 1) * D_V] = hg
            c_ref[s, h] = c_new
            n_ref[s, h:h + 1, :] = n_new
            mrow = jnp.where(lane == h, m_new, mrow)
        m_ref[s] = mrow


def _mlstm_sample_call(z_main, z_if, bias, hnorm, c0, n0, m0):
    nb = SAMPLE_SEQS_PER_STEP
    rows = nb * DEC_SEQ
    base = T_P // rows
    blk = lambda w, c: pl.BlockSpec((rows, w), lambda g, c=c: (base + g, c))
    return pl.pallas_call(
        _mlstm_sample_kernel,
        grid=(DEC_BATCH // nb,),
        in_specs=[
            blk(D_QKB, 3), blk(D_QKB, 4), blk(D_QKB, 5), blk(D_QKB, 6), blk(D_QKB, 7),
            blk(D_QKB, 8), blk(LANES, 0),
            pl.BlockSpec((1, LANES), lambda g: (0, 0)),
            pl.BlockSpec((1, D_VB), lambda g: (0, 0)),
            pl.BlockSpec((nb, N_HEADS, D_QK, D_V), lambda g: (g, 0, 0, 0)),
            pl.BlockSpec((nb, N_HEADS, D_QK), lambda g: (g, 0, 0)),
            pl.BlockSpec((nb, 1, LANES), lambda g: (g, 0, 0)),
        ],
        out_specs=[
            pl.BlockSpec((rows, D_VB), lambda g: (g, 0)),
            pl.BlockSpec((nb, N_HEADS, D_QK, D_V), lambda g: (g, 0, 0, 0)),
            pl.BlockSpec((nb, N_HEADS, D_QK), lambda g: (g, 0, 0)),
            pl.BlockSpec((nb, 1, LANES), lambda g: (g, 0, 0)),
        ],
        out_shape=[
            jax.ShapeDtypeStruct((T_S, D_VB), F32),
            jax.ShapeDtypeStruct((DEC_BATCH, N_HEADS, D_QK, D_V), F32),
            jax.ShapeDtypeStruct((DEC_BATCH, N_HEADS, D_QK), F32),
            jax.ShapeDtypeStruct((DEC_BATCH, 1, LANES), F32),
        ],
        compiler_params=_params(1),
        name="mlstm_sample",
    )(z_main, z_main, z_main, z_main, z_main, z_main, z_if, bias, hnorm, c0, n0, m0)


NP_MIX = T_P // TM_MIX
TILES_PER_SEQ = SEQ // TM_MIX


def _mixer_kernel(cb_ref, cc_ref, cx_ref, ccp_ref, cxp_ref, buf_ref, hgp_ref, hgs_ref, ga_ref,
                  gb_ref, cw_ref, wa_ref, wb_ref, mix_ref, utail_ref, us_ref, a_ref):
    i = pl.program_id(0)
    u = cc_ref[...].astype(F32) * cx_ref[...].astype(F32)
    w0 = cw_ref[0:1, :]
    w1 = cw_ref[1:2, :]
    w2 = cw_ref[2:3, :]
    rowid = lax.broadcasted_iota(jnp.int32, u.shape, 0)
    um1 = pltpu.roll(u, 1, 0)
    um2 = pltpu.roll(u, 2, 0)
    cb = cb_ref[...].astype(F32)

    @pl.when(i < NP_MIX)
    def _():
        valid = (i % TILES_PER_SEQ != 0).astype(F32)
        uprev = ccp_ref[...].astype(F32) * cxp_ref[...].astype(F32) * valid
        p1 = uprev[15:16, :]
        p2 = uprev[14:15, :]
        m1 = jnp.where(rowid == 0, p1, um1)
        m2 = jnp.where(rowid == 0, p2, jnp.where(rowid == 1, p1, um2))
        a_ref[...] = (cb * (w0 * m2 + w1 * m1 + w2 * u)).astype(BF16)

        @pl.when(i % TILES_PER_SEQ == TILES_PER_SEQ - 1)
        def _():
            utail_ref[...] = u[TM_MIX - 8:, :]

    @pl.when(i >= NP_MIX)
    def _():
        pos = rowid % DEC_SEQ
        bp = buf_ref[...]
        m1 = jnp.where(pos >= 1, um1, pltpu.roll(bp, TM_MIX - 1, 0))
        m2 = jnp.where(pos >= 2, um2, bp)
        a_ref[...] = (cb * (w0 * m2 + w1 * m1 + w2 * u)).astype(BF16)
        us_ref[...] = u

    ya = _dot(a_ref[...], wa_ref[...])

    def finish(hg):
        yb = _dot(hg, wb_ref[...])
        mix = (jax.nn.sigmoid(ga_ref[...].astype(F32)) * ya
               + jax.nn.sigmoid(gb_ref[...].astype(F32)) * yb)
        mix_ref[...] = mix.astype(mix_ref.dtype)

    @pl.when(i < NP_MIX)
    def _():
        finish(hgp_ref[...])

    @pl.when(i >= NP_MIX)
    def _():
        finish(hgs_ref[...].astype(BF16))


def _mixer_call(z_main, z_gab, bufpad, hg_p, hg_s, conv_w, wa, wb):
    tm = TM_MIX
    pidx = lambda i: jnp.minimum(i, NP_MIX - 1)
    sidx = lambda i: jnp.maximum(i - NP_MIX, 0)
    return pl.pallas_call(
        _mixer_kernel,
        grid=(T_ALL // tm,),
        in_specs=[
            pl.BlockSpec((tm, D_CONV), lambda i: (i, 0)),
            pl.BlockSpec((tm, D_CONV), lambda i: (i, 1)),
            pl.BlockSpec((tm, D_CONV), lambda i: (i, 2)),
            pl.BlockSpec((16, D_CONV), lambda i: (jnp.maximum(i * (tm // 16) - 1, 0), 1)),
            pl.BlockSpec((16, D_CONV), lambda i: (jnp.maximum(i * (tm // 16) - 1, 0), 2)),
            pl.BlockSpec((tm, D_CONV), lambda i: (sidx(i), 0)),
            pl.BlockSpec((tm, D_VB), lambda i: (pidx(i), 0)),
            pl.BlockSpec((tm, D_VB), lambda i: (sidx(i), 0)),
            pl.BlockSpec((tm, D_MODEL), lambda i: (i, 0)),
            pl.BlockSpec((tm, D_MODEL), lambda i: (i, 1)),
            pl.BlockSpec((3, D_CONV), lambda i: (0, 0)),
            pl.BlockSpec((D_CONV, D_MODEL), lambda i: (0, 0)),
            pl.BlockSpec((D_VB, D_MODEL), lambda i: (0, 0)),
        ],
        out_specs=[
            pl.BlockSpec((tm, D_MODEL), lambda i: (i, 0)),
            pl.BlockSpec((8, D_CONV), lambda i: (jnp.minimum(i // TILES_PER_SEQ, BATCH - 1), 0)),
            pl.BlockSpec((tm, D_CONV), lambda i: (sidx(i), 0)),
        ],
        out_shape=[
            jax.ShapeDtypeStruct((T_ALL, D_MODEL), BF16),
            jax.ShapeDtypeStruct((BATCH * 8, D_CONV), F32),
            jax.ShapeDtypeStruct((T_S, D_CONV), F32),
        ],
        scratch_shapes=[pltpu.VMEM((tm, D_CONV), BF16)],
        compiler_params=_params(1),
        name="mixer",
    )(z_main, z_main, z_main, z_main, z_main, bufpad, hg_p, hg_s, z_gab, z_gab, conv_w, wa, wb)


RT_E1, RT_E2, RT_V1, RT_V2, RT_P1, RT_P2 = range(6)
ST_CNT, ST_CUR, ST_FREE = range(3)


def _scatter_rows(hbuf, slot, pos_sm, xs_hbm, sem):
    for k in range(TOP_K):
        for r in range(TM_MIX):
            pltpu.make_async_copy(hbuf.at[slot, pl.ds(r, 1), :],
                                  xs_hbm.at[pl.ds(pos_sm[slot, k, r], 1), :],
                                  sem.at[slot]).start()


def _scatter_wait(hbuf, slot, xs_hbm, sem):
    for _ in range(TOP_K):
        pltpu.make_async_copy(xs_hbm.at[pl.ds(0, TM_MIX), :], hbuf.at[slot], sem.at[slot]).wait()


def _router_kernel(mix_ref, xp_ref, xs_ref, wo_ref, g2_ref, wr_hi_ref, wr_lo_ref,
                   x2_ref, rt_ref, tlog_ref, xs_hbm,
                   hbuf, st_ref, pos_vm, pos_sm, st_vm, st_sm, zt_ref,
                   sem_row, sem_pos, sem_st, sem_fill):
    i = pl.program_id(0)
    n = pl.num_programs(0)
    slot = i % 2
    tm = TM_MIX

    @pl.when(i == 0)
    def _():
        st_ref[...] = jnp.zeros_like(st_ref)
        zt_ref[...] = jnp.zeros_like(zt_ref)

    @pl.when(i >= 1)
    def _():
        pltpu.make_async_copy(pos_vm.at[1 - slot], pos_sm.at[1 - slot], sem_pos.at[1 - slot]).wait()
        _scatter_rows(hbuf, 1 - slot, pos_sm, xs_hbm, sem_row)

    @pl.when(i >= 2)
    def _():
        _scatter_wait(hbuf, slot, xs_hbm, sem_row)

    x2_ref[...] = _dot(mix_ref[...], wo_ref[...])

    @pl.when(i < NP_MIX)
    def _():
        x2_ref[...] += xp_ref[...]

    @pl.when(i >= NP_MIX)
    def _():
        x2_ref[...] += xs_ref[...]

    x2 = x2_ref[...]
    hn2 = x2 * lax.rsqrt(jnp.mean(x2 * x2, axis=-1, keepdims=True) + EPS) * g2_ref[...]
    hbuf[slot] = hn2
    hi = hn2.astype(BF16)
    lo = (hn2 - hi.astype(F32)).astype(BF16)
    logits = _dot(hi, wr_hi_ref[...]) + (_dot(lo, wr_hi_ref[...]) + _dot(hi, wr_lo_ref[...]))
    lane = lax.broadcasted_iota(jnp.int32, logits.shape, 1)
    lanef = lane.astype(F32)
    neg = -jnp.inf
    gl = jnp.where(lane < N_GROUPS, logits, neg)
    gmax = jnp.max(gl, axis=-1, keepdims=True)
    g_star = jnp.min(jnp.where(gl == gmax, lanef, float(LANES)), axis=-1, keepdims=True)
    p_grp = 1.0 / jnp.sum(jnp.exp(gl - gmax), axis=-1, keepdims=True)
    lo_lane = N_GROUPS + EXPERTS_PER_GROUP * g_star
    el = jnp.where((lanef >= lo_lane) & (lanef < lo_lane + EXPERTS_PER_GROUP), logits, neg)
    e1 = jnp.max(el, axis=-1, keepdims=True)
    i1 = jnp.min(jnp.where(el == e1, lanef, float(LANES)), axis=-1, keepdims=True)
    el2 = jnp.where(lanef == i1, neg, el)
    e2 = jnp.max(el2, axis=-1, keepdims=True)
    i2 = jnp.min(jnp.where(el2 == e2, lanef, float(LANES)), axis=-1, keepdims=True)
    t = jnp.exp(e2 - e1)
    v1 = p_grp / (1.0 + t)
    v2 = v1 * t
    id1 = i1 - N_GROUPS
    id2 = i2 - N_GROUPS

    m1 = (lanef == id1).astype(F32)
    m2 = (lanef == id2).astype(F32)
    mm = m1 + m2
    row = lax.broadcasted_iota(jnp.int32, (tm, tm), 0)
    col = lax.broadcasted_iota(jnp.int32, (tm, tm), 1)
    earlier = _dot((row > col).astype(BF16), mm.astype(BF16))
    c_row = jnp.sum(mm, axis=0, keepdims=True)
    st = st_ref[...]
    cnt = st[ST_CNT:ST_CNT + 1, :]
    cur = st[ST_CUR:ST_CUR + 1, :]
    free = st[ST_FREE:ST_FREE + 1, :]
    inv = 1.0 / TM_EXP
    q0 = jnp.floor(cnt * inv)
    rem = cnt - TM_EXP * q0
    need0 = (c_row > 0) & (rem == 0)
    need1 = (rem > 0) & (rem + c_row > TM_EXP)
    alloc = (need0 | need1).astype(F32)
    er = lax.broadcasted_iota(jnp.int32, (LANES, LANES), 0)
    ec = lax.broadcasted_iota(jnp.int32, (LANES, LANES), 1)
    before = _dot(jnp.broadcast_to(alloc, (8, LANES)).astype(BF16), (er < ec).astype(BF16))[0:1, :]
    newtile = free + before
    tile_a = jnp.where(rem > 0, cur, newtile)
    thr = TM_EXP * (q0 + 1.0)
    base_a = TM_EXP * (tile_a - q0)
    base_b = TM_EXP * (newtile - q0 - 1.0)
    rank = earlier + cnt
    posfull = rank + jnp.where(rank < thr, base_a, base_b)
    pos1 = jnp.sum(m1 * posfull, axis=-1, keepdims=True)
    pos2 = jnp.sum(m2 * posfull, axis=-1, keepdims=True)
    cnt_new = cnt + c_row
    cur_new = jnp.where(need1, newtile, tile_a)
    free_new = free + jnp.sum(alloc, axis=-1, keepdims=True)

    rt = jnp.where(lane == RT_E1, id1,
         jnp.where(lane == RT_E2, id2,
         jnp.where(lane == RT_V1, v1,
         jnp.where(lane == RT_V2, v2,
         jnp.where(lane == RT_P1, pos1, jnp.where(lane == RT_P2, pos2, 0.0))))))
    rt_ref[...] = rt
    sub = lax.broadcasted_iota(jnp.int32, (8, LANES), 0)
    st_new = jnp.where(sub == ST_CNT, cnt_new,
             jnp.where(sub == ST_CUR, cur_new, jnp.where(sub == ST_FREE, free_new, 0.0)))
    st_ref[...] = st_new
    tlog_ref[0] = jnp.where(sub == 0, newtile, jnp.where(sub == 1, alloc, 0.0))

    pick = ((sub == 0) & (lax.broadcasted_iota(jnp.int32, (8, LANES), 1) == RT_P1)) | (
        (sub == 1) & (lax.broadcasted_iota(jnp.int32, (8, LANES), 1) == RT_P2))
    pick = pick.astype(BF16)
    rt_hi = rt.astype(BF16)
    rt_lo = (rt - rt_hi.astype(F32)).astype(BF16)
    pos_vm[slot] = (_dot_nt(pick, rt_hi) + _dot_nt(pick, rt_lo)).astype(jnp.int32)
    pltpu.make_async_copy(pos_vm.at[slot], pos_sm.at[slot], sem_pos.at[slot]).start()

    @pl.when(i == n - 1)
    def _():
        pltpu.make_async_copy(pos_vm.at[slot], pos_sm.at[slot], sem_pos.at[slot]).wait()
        _scatter_rows(hbuf, slot, pos_sm, xs_hbm, sem_row)
        st_vm[...] = st_new.astype(jnp.int32)
        pltpu.make_async_copy(st_vm, st_sm, sem_st).start()
        pltpu.make_async_copy(st_vm, st_sm, sem_st).wait()

        def fill_row(p):
            return pltpu.make_async_copy(zt_ref.at[pl.ds(0, 1), :], xs_hbm.at[pl.ds(p, 1), :], sem_fill)

        def fill_tile(tile):
            return pltpu.make_async_copy(zt_ref, xs_hbm.at[pl.ds(tile * TM_EXP, TM_EXP), :], sem_fill)

        for e in range(N_EXPERTS):
            r0 = st_sm[ST_CNT, e] % TM_EXP
            n_tail = (TM_EXP - r0) % TM_EXP
            p0 = st_sm[ST_CUR, e] * TM_EXP + r0
            lax.fori_loop(0, n_tail, lambda r, c: (fill_row(p0 + r).start(), c)[1], 0)
        first_free = st_sm[ST_FREE, 0]
        lax.fori_loop(first_free, NT_EXP, lambda tl, c: (fill_tile(tl).start(), c)[1], 0)
        for e in range(N_EXPERTS):
            n_tail = (TM_EXP - st_sm[ST_CNT, e] % TM_EXP) % TM_EXP
            lax.fori_loop(0, n_tail, lambda r, c: (fill_row(0).wait(), c)[1], 0)
        lax.fori_loop(first_free, NT_EXP, lambda tl, c: (fill_tile(0).wait(), c)[1], 0)
        _scatter_wait(hbuf, 1 - slot, xs_hbm, sem_row)
        _scatter_wait(hbuf, slot, xs_hbm, sem_row)


def _router_call(mix, xp, xs, wo, g2, wr_hi, wr_lo):
    tm = TM_MIX
    n = T_ALL // tm
    pidx = lambda i: jnp.minimum(i, NP_MIX - 1)
    sidx = lambda i: jnp.maximum(i - NP_MIX, 0)
    return pl.pallas_call(
        _router_kernel,
        grid=(n,),
        in_specs=[
            pl.BlockSpec((tm, D_MODEL), lambda i: (i, 0)),
            pl.BlockSpec((tm, D_MODEL), lambda i: (pidx(i), 0)),
            pl.BlockSpec((tm, D_MODEL), lambda i: (sidx(i), 0)),
            pl.BlockSpec((D_MODEL, D_MODEL), lambda i: (0, 0)),
            pl.BlockSpec((1, D_MODEL), lambda i: (0, 0)),
            pl.BlockSpec((D_MODEL, LANES), lambda i: (0, 0)),
            pl.BlockSpec((D_MODEL, LANES), lambda i: (0, 0)),
        ],
        out_specs=[
            pl.BlockSpec((tm, D_MODEL), lambda i: (i, 0)),
            pl.BlockSpec((tm, LANES), lambda i: (i, 0)),
            pl.BlockSpec((1, 8, LANES), lambda i: (i, 0, 0)),
            pl.BlockSpec(memory_space=pl.ANY),
        ],
        out_shape=[
            jax.ShapeDtypeStruct((T_ALL, D_MODEL), F32),
            jax.ShapeDtypeStruct((T_ALL, LANES), F32),
            jax.ShapeDtypeStruct((n, 8, LANES), F32),
            jax.ShapeDtypeStruct((R_EXP, D_MODEL), F32),
        ],
        scratch_shapes=[
            pltpu.VMEM((2, tm, D_MODEL), F32),
            pltpu.VMEM((8, LANES), F32),
            pltpu.VMEM((2, 8, tm), jnp.int32),
            pltpu.SMEM((2, 8, tm), jnp.int32),
            pltpu.VMEM((8, LANES), jnp.int32),
            pltpu.SMEM((8, LANES), jnp.int32),
            pltpu.VMEM((TM_EXP, D_MODEL), F32),
            pltpu.SemaphoreType.DMA((2,)),
            pltpu.SemaphoreType.DMA((2,)),
            pltpu.SemaphoreType.DMA(()),
            pltpu.SemaphoreType.DMA(()),
        ],
        compiler_params=_params(1),
        name="router",
    )(mix, xp, xs, wo, g2, wr_hi, wr_lo)


def _row_copy(src_hbm, row, dst_ref, slot, r, sem):
    return pltpu.make_async_copy(
        src_hbm.at[pl.ds(row, 1), :], dst_ref.at[slot, pl.ds(r, 1), :], sem.at[slot])


def _gather_start(src_hbm, idx_ref, dst_ref, slot, sem, n_rows, first=0, two_queues=False):
    for r in range(first, first + n_rows):
        _row_copy(src_hbm, idx_ref[0, 0, r], dst_ref, slot, r, sem).start(
            priority=r % 2 if two_queues else 0)


def _gather_wait(src_hbm, dst_ref, slot, sem, n_rows):
    pltpu.make_async_copy(src_hbm.at[pl.ds(0, n_rows), :], dst_ref.at[slot], sem.at[slot]).wait()


def _experts_kernel(tid_ref, te_ref, na_ref, x_ref, wg_ref, wu_ref, wd_ref, o_ref):
    j = pl.program_id(0)
    n_active = na_ref[0]

    @pl.when(j < n_active)
    def _():
        x = x_ref[...].astype(BF16)
        a = _dot(x, wg_ref[0])
        u = _dot(x, wu_ref[0])
        act = (a * jax.nn.sigmoid(a) * u).astype(BF16)
        o_ref[...] = _dot(act, wd_ref[0])

    @pl.when(j >= n_active)
    def _():
        o_ref[...] = jnp.zeros_like(o_ref)


def _experts_call(tile_id, tile_expert, n_active, x_sorted, wg, wu, wd):
    grid_spec = pltpu.PrefetchScalarGridSpec(
        num_scalar_prefetch=3,
        grid=(NT_EXP,),
        in_specs=[
            pl.BlockSpec((TM_EXP, D_MODEL), lambda j, tid, te, na: (tid[j], 0)),
            pl.BlockSpec((1, D_MODEL, D_EXPERT), lambda j, tid, te, na: (te[j], 0, 0)),
            pl.BlockSpec((1, D_MODEL, D_EXPERT), lambda j, tid, te, na: (te[j], 0, 0)),
            pl.BlockSpec((1, D_EXPERT, D_MODEL), lambda j, tid, te, na: (te[j], 0, 0)),
        ],
        out_specs=pl.BlockSpec((TM_EXP, D_MODEL), lambda j, tid, te, na: (tid[j], 0)),
    )
    return pl.pallas_call(
        _experts_kernel,
        grid_spec=grid_spec,
        out_shape=jax.ShapeDtypeStruct((R_EXP, D_MODEL), F32),
        compiler_params=_params(1),
        name="experts",
    )(tile_id, tile_expert, n_active, x_sorted, wg, wu, wd)


def _combine_kernel(p1c_ref, p1n_ref, p2c_ref, p2n_ref, eo_hbm, x2_ref, rt_ref, gf_ref,
                    yp_ref, ys_ref, buf1, buf2, sem1, sem2):
    i = pl.program_id(0)
    n = pl.num_programs(0)
    slot = i % 2

    @pl.when(i == 0)
    def _():
        _gather_start(eo_hbm, p1c_ref, buf1, 0, sem1, TM_MIX, two_queues=True)
        _gather_start(eo_hbm, p2c_ref, buf2, 0, sem2, TM_MIX, two_queues=True)

    @pl.when(i + 1 < n)
    def _():
        _gather_start(eo_hbm, p1n_ref, buf1, 1 - slot, sem1, TM_MIX, two_queues=True)
        _gather_start(eo_hbm, p2n_ref, buf2, 1 - slot, sem2, TM_MIX, two_queues=True)

    _gather_wait(eo_hbm, buf1, slot, sem1, TM_MIX)
    _gather_wait(eo_hbm, buf2, slot, sem2, TM_MIX)
    rt = rt_ref[...]
    y = x2_ref[...] + rt[:, 2:3] * buf1[slot] + rt[:, 3:4] * buf2[slot]
    out = y * lax.rsqrt(jnp.mean(y * y, axis=-1, keepdims=True) + EPS) * gf_ref[...]

    @pl.when(i < NP_MIX)
    def _():
        yp_ref[...] = out

    @pl.when(i >= NP_MIX)
    def _():
        ys_ref[...] = out


def _combine_call(pos1, pos2, eo, x2, rt, gf):
    tm = TM_MIX
    n = T_ALL // tm
    cur = lambda i: (i, 0, 0)
    nxt = lambda i: (jnp.minimum(i + 1, n - 1), 0, 0)
    smem = lambda f: pl.BlockSpec((1, 1, tm), f, memory_space=pltpu.SMEM)
    return pl.pallas_call(
        _combine_kernel,
        grid=(n,),
        in_specs=[
            smem(lambda i: (0, 0, 0)), smem(nxt), smem(lambda i: (0, 0, 0)), smem(nxt),
            pl.BlockSpec(memory_space=pl.ANY),
            pl.BlockSpec((tm, D_MODEL), lambda i: (i, 0)),
            pl.BlockSpec((tm, LANES), lambda i: (i, 0)),
            pl.BlockSpec((1, D_MODEL), lambda i: (0, 0)),
        ],
        out_specs=[
            pl.BlockSpec((tm, D_MODEL), lambda i: (jnp.minimum(i, NP_MIX - 1), 0)),
            pl.BlockSpec((tm, D_MODEL), lambda i: (jnp.maximum(i - NP_MIX, 0), 0)),
        ],
        out_shape=[
            jax.ShapeDtypeStruct((T_P, D_MODEL), F32),
            jax.ShapeDtypeStruct((T_S, D_MODEL), F32),
        ],
        scratch_shapes=[
            pltpu.VMEM((2, tm, D_MODEL), F32),
            pltpu.VMEM((2, tm, D_MODEL), F32),
            pltpu.SemaphoreType.DMA((2,)),
            pltpu.SemaphoreType.DMA((2,)),
        ],
        compiler_params=_params(1),
        name="combine",
    )(pos1, pos1, pos2, pos2, eo, x2, rt, gf)


def _tile_schedule(tlog):
    newtile = tlog[:, 0, :N_EXPERTS].astype(jnp.int32)
    alloc = tlog[:, 1, :N_EXPERTS] > 0.5
    expert = jnp.broadcast_to(jnp.arange(N_EXPERTS, dtype=jnp.int32)[None, :], newtile.shape)
    owner = jnp.full((NT_EXP,), N_EXPERTS, jnp.int32).at[
        jnp.where(alloc, newtile, NT_EXP).reshape(-1)].set(expert.reshape(-1), mode="drop")
    tile_id = jnp.argsort(owner, stable=True).astype(jnp.int32)
    owner_sorted = owner[tile_id]
    n_active = jnp.sum((owner < N_EXPERTS).astype(jnp.int32))
    last = owner_sorted[jnp.maximum(n_active - 1, 0)]
    tile_expert = jnp.where(jnp.arange(NT_EXP) < n_active, owner_sorted, last).astype(jnp.int32)
    return tile_id, tile_expert, n_active.reshape(1)


def kernel(x_prompt, x_sample, state_conv, state_C, state_n, state_m, norm1, w_in, b_if, conv_w,
           head_norm, w_a_out, w_b_out, w_out, norm2, w_grp, w_rt, w_expert_gate, w_expert_up,
           w_expert_down, norm_final):
    xp = x_prompt.reshape(T_P, D_MODEL)
    xs = x_sample.reshape(T_S, D_MODEL)
    w_in_t = jnp.swapaxes(w_in[0], 0, 1)
    w_gab = w_in_t[N_MAIN + N_IF:]
    w_if = jnp.pad(w_in_t[N_MAIN:N_MAIN + N_IF], ((0, LANES - N_IF), (0, 0)))
    bias = jnp.pad(b_if[0].astype(F32), (0, LANES - N_IF)).reshape(1, LANES)

    hn = _norm_call(xp, xs, norm1[0].reshape(1, D_MODEL))
    z_main = _proj_call(hn, w_in_t, N_MAIN, TN_PROJ, BF16, "inproj_main")
    z_gab = _proj_call(hn, w_gab, 2 * D_MODEL, TN_PROJ, BF16, "inproj_gates")
    z_if = _proj_call(hn, w_if, LANES, LANES, F32, "inproj_if")

    hnorm = head_norm[0].astype(F32).reshape(1, D_VB)
    hg_p, c_p, n_p, m_p = _mlstm_prompt_call(z_main, z_if, bias, hnorm)
    m0 = jnp.pad(state_m[0].astype(F32), ((0, 0), (0, LANES - N_HEADS))).reshape(DEC_BATCH, 1, LANES)
    hg_s, c_s, n_s, m_s = _mlstm_sample_call(
        z_main, z_if, bias, hnorm, state_C[0].astype(F32), state_n[0].astype(F32), m0)

    bufpad = jnp.pad(state_conv[0].astype(F32), ((0, 0), (0, DEC_SEQ - 2), (0, 0))).reshape(T_S, D_CONV)
    mix, utail, us = _mixer_call(z_main, z_gab, bufpad, hg_p, hg_s, conv_w[0],
                                 w_a_out[0].astype(BF16), w_b_out[0].astype(BF16))

    wr = jnp.pad(jnp.concatenate([w_grp[0], w_rt[0]], axis=1),
                 ((0, 0), (0, LANES - N_GROUPS - N_EXPERTS)))
    wr_hi = wr.astype(BF16)
    wr_lo = (wr - wr_hi.astype(F32)).astype(BF16)
    x2, rt, tlog, x_sorted = _router_call(mix, xp, xs, w_out[0].astype(BF16),
                                          norm2[0].reshape(1, D_MODEL), wr_hi, wr_lo)

    tile_id, tile_expert, n_active = _tile_schedule(tlog)
    eo = _experts_call(tile_id, tile_expert, n_active, x_sorted, w_expert_gate[0].astype(BF16),
                       w_expert_up[0].astype(BF16), w_expert_down[0].astype(BF16))
    pos = rt[:, RT_P1:RT_P2 + 1].astype(jnp.int32)
    pos1 = pos[:, 0].reshape(T_ALL // TM_MIX, 1, TM_MIX)
    pos2 = pos[:, 1].reshape(T_ALL // TM_MIX, 1, TM_MIX)
    y_p, y_s = _combine_call(pos1, pos2, eo, x2, rt, norm_final.reshape(1, D_MODEL))

    new_conv_p = utail.reshape(BATCH, 8, D_CONV)[:, 6:8, :][None]
    new_conv_s = us.reshape(DEC_BATCH, DEC_SEQ, D_CONV)[:, DEC_SEQ - 2:, :][None]
    return (y_p.reshape(BATCH, SEQ, D_MODEL), y_s.reshape(DEC_BATCH, DEC_SEQ, D_MODEL),
            new_conv_p, c_p[None], n_p[None], m_p[:, 0, :N_HEADS][None],
            new_conv_s, c_s[None], n_s[None], m_s[:, 0, :N_HEADS][None])
```

```python
import functools

import jax
import jax.numpy as jnp
from jax import lax
from jax.experimental import pallas as pl
from jax.experimental.pallas import tpu as pltpu

F32 = jnp.float32
BF16 = jnp.bfloat16

D_MODEL = 2048
BATCH = 4
SEQ = 2048
DEC_BATCH = 128
DEC_SEQ = 8
D_CONV = 1024
N_HEADS = 4
D_QK = 256
D_V = 512
D_QKB = N_HEADS * D_QK
D_VB = N_HEADS * D_V
CHUNK = 128
N_GROUPS = 4
EXPERTS_PER_GROUP = 4
N_EXPERTS = 16
TOP_K = 2
D_EXPERT = 1024
EPS = 1e-6

T_P = BATCH * SEQ
T_S = DEC_BATCH * DEC_SEQ
T_ALL = T_P + T_S
N_MAIN = 3 * D_CONV + 2 * D_QKB + 2 * D_VB
N_IF = 2 * N_HEADS
LANES = 128
VMEM_LIMIT = 56 * 1024 * 1024

TM_NORM = 512
TM_PROJ = 1024
TN_PROJ = 1024
TM_MIX = 256
TM_EXP = 256
N_ASSIGN = T_ALL * TOP_K
NT_EXP = N_ASSIGN // TM_EXP + N_EXPERTS
R_EXP = NT_EXP * TM_EXP
SAMPLE_SEQS_PER_STEP = 2


def _params(n_axes):
    return pltpu.CompilerParams(
        dimension_semantics=("arbitrary",) * n_axes, vmem_limit_bytes=VMEM_LIMIT)


def _split3(x):
    x1 = x.astype(BF16)
    r1 = x - x1.astype(F32)
    x2 = r1.astype(BF16)
    x3 = (r1 - x2.astype(F32)).astype(BF16)
    return x1, x2, x3


def _dot(a, b):
    return jnp.dot(a, b, preferred_element_type=F32)


def _dot_nt(a, b):
    return lax.dot_general(a, b, (((1,), (1,)), ((), ())), preferred_element_type=F32)


def _dot_tn(a, b):
    return lax.dot_general(a, b, (((0,), (0,)), ((), ())), preferred_element_type=F32)


def _norm_kernel(xp_ref, xs_ref, g_ref, o_ref):
    i = pl.program_id(0)

    def emit(x):
        r = lax.rsqrt(jnp.mean(x * x, axis=-1, keepdims=True) + EPS)
        o_ref[...] = (x * r * g_ref[...]).astype(o_ref.dtype)

    @pl.when(i < T_P // TM_NORM)
    def _():
        emit(xp_ref[...])

    @pl.when(i >= T_P // TM_NORM)
    def _():
        emit(xs_ref[...])


def _norm_call(xp, xs, g):
    npt = T_P // TM_NORM
    return pl.pallas_call(
        _norm_kernel,
        grid=(T_ALL // TM_NORM,),
        in_specs=[
            pl.BlockSpec((TM_NORM, D_MODEL), lambda i: (jnp.minimum(i, npt - 1), 0)),
            pl.BlockSpec((TM_NORM, D_MODEL), lambda i: (jnp.maximum(i - npt, 0), 0)),
            pl.BlockSpec((1, D_MODEL), lambda i: (0, 0)),
        ],
        out_specs=pl.BlockSpec((TM_NORM, D_MODEL), lambda i: (i, 0)),
        out_shape=jax.ShapeDtypeStruct((T_ALL, D_MODEL), BF16),
        compiler_params=_params(1),
        name="norm",
    )(xp, xs, g)


def _proj_kernel(h_ref, w_ref, o_ref, wb_ref):
    @pl.when(pl.program_id(1) == 0)
    def _():
        wb_ref[...] = w_ref[...].astype(BF16)

    o_ref[...] = _dot_nt(h_ref[...], wb_ref[...]).astype(o_ref.dtype)


def _proj_call(h, wt, n_cols, tn, out_dtype, name):
    return pl.pallas_call(
        _proj_kernel,
        grid=(n_cols // tn, T_ALL // TM_PROJ),
        in_specs=[
            pl.BlockSpec((TM_PROJ, D_MODEL), lambda j, i: (i, 0)),
            pl.BlockSpec((tn, D_MODEL), lambda j, i: (j, 0)),
        ],
        out_specs=pl.BlockSpec((TM_PROJ, tn), lambda j, i: (i, j)),
        out_shape=jax.ShapeDtypeStruct((T_ALL, n_cols), out_dtype),
        scratch_shapes=[pltpu.VMEM((tn, D_MODEL), BF16)],
        compiler_params=_params(2),
        name=name,
    )(h, wt)


def _gate_prep(L, gates, bias):
    g = gates + bias
    lane = lax.broadcasted_iota(jnp.int32, g.shape, 1)
    logsig = jnp.minimum(g, 0.0) - jnp.log(1.0 + jnp.exp(-jnp.abs(g)))
    igf = jnp.where(lane < N_HEADS, g, logsig)
    row = lax.broadcasted_iota(jnp.int32, (L, L), 0)
    col = lax.broadcasted_iota(jnp.int32, (L, L), 1)
    tril = (row >= col).astype(BF16)
    triu = (row <= col).astype(BF16)
    eye8 = (lax.broadcasted_iota(jnp.int32, (8, LANES), 0)
            == lax.broadcasted_iota(jnp.int32, (8, LANES), 1)).astype(BF16)
    parts = _split3(igf)
    bcum = sum(_dot(tril, p) for p in parts)
    gt = sum(_dot_nt(eye8, p) for p in parts)
    bt = sum(_dot(p, triu) for p in _split3(gt))
    return igf, bcum, gt, bt


def _mlstm_head(L, h, q, k, v, zo, prep, m_prev, c_prev, n_prev, hnorm):
    igf, bcum, gt, bt = prep
    bcol = bcum[:, N_HEADS + h:N_HEADS + h + 1]
    brow = bt[N_HEADS + h:N_HEADS + h + 1, :]
    igrow = gt[h:h + 1, :]
    igcol = igf[:, h:h + 1]
    row = lax.broadcasted_iota(jnp.int32, (L, L), 0)
    col = lax.broadcasted_iota(jnp.int32, (L, L), 1)
    dmat = jnp.where(row >= col, bcol - brow + igrow, -jnp.inf)
    log_inter = bcol + m_prev
    m_tok = jnp.maximum(log_inter, jnp.max(dmat, axis=-1, keepdims=True))
    w_inter = jnp.exp(log_inter - m_tok)
    scale = D_QK ** -0.5
    qb = q.astype(BF16)
    kb = k.astype(BF16)
    vb = v.astype(BF16)
    qf = q.astype(F32)
    kf = k.astype(F32)
    s = jnp.exp(dmat - m_tok) * (_dot_nt(qb, kb) * scale)
    num = (w_inter * scale) * _dot(qb, c_prev.astype(BF16)) + _dot(s.astype(BF16), vb)
    qn = jnp.sum(qf * n_prev, axis=-1, keepdims=True) * scale
    den = w_inter * qn + jnp.sum(s, axis=-1, keepdims=True)
    hout = num / jnp.maximum(jnp.abs(den), jnp.exp(-m_tok))
    b_last = bcol[L - 1:L, :]
    log_kv = b_last - bcol + igcol
    m_new = jnp.maximum(b_last + m_prev, jnp.max(log_kv, axis=0, keepdims=True))
    f_st = jnp.exp(b_last + m_prev - m_new)
    kw = kf * jnp.exp(log_kv - m_new)
    c_new = f_st * c_prev + _dot_tn(kw.astype(BF16), vb)
    n_new = f_st * n_prev + jnp.sum(kw, axis=0, keepdims=True)
    hn = hout * lax.rsqrt(jnp.mean(hout * hout, axis=-1, keepdims=True) + EPS) * hnorm
    hg = jax.nn.sigmoid(zo.astype(F32)) * hn
    return hg, c_new, n_new, m_new


def _mlstm_prompt_kernel(q_ref, k_ref, v0_ref, v1_ref, zo0_ref, zo1_ref, g_ref, bias_ref, hn_ref,
                         hg_ref, c_ref, n_ref, m_ref):
    L = CHUNK

    @pl.when(pl.program_id(1) == 0)
    def _():
        c_ref[...] = jnp.zeros_like(c_ref)
        n_ref[...] = jnp.zeros_like(n_ref)
        m_ref[...] = jnp.zeros_like(m_ref)

    prep = _gate_prep(L, g_ref[...], bias_ref[...])
    mrow = m_ref[0]
    lane = lax.broadcasted_iota(jnp.int32, mrow.shape, 1)
    v_refs = (v0_ref, v1_ref)
    zo_refs = (zo0_ref, zo1_ref)
    for h in range(N_HEADS):
        hv = (h % 2) * D_V
        hg, c_new, n_new, m_new = _mlstm_head(
            L, h,
            q_ref[:, h * D_QK:(h + 1) * D_QK], k_ref[:, h * D_QK:(h + 1) * D_QK],
            v_refs[h // 2][:, hv:hv + D_V], zo_refs[h // 2][:, hv:hv + D_V],
            prep, mrow[:, h:h + 1], c_ref[0, h], n_ref[0, h:h + 1, :],
            hn_ref[:, h * D_V:(h + 1) * D_V])
        hg_ref[:, h * D_V:(h + 1) * D_V] = hg.astype(hg_ref.dtype)
        c_ref[0, h] = c_new
        n_ref[0, h:h + 1, :] = n_new
        mrow = jnp.where(lane == h, m_new, mrow)
    m_ref[0] = mrow


def _mlstm_prompt_call(z_main, z_if, bias, hnorm):
    nc = SEQ // CHUNK
    blk = lambda w, c: pl.BlockSpec((CHUNK, w), lambda b, t, c=c: (b * nc + t, c))
    return pl.pallas_call(
        _mlstm_prompt_kernel,
        grid=(BATCH, nc),
        in_specs=[
            blk(D_QKB, 3), blk(D_QKB, 4), blk(D_QKB, 5), blk(D_QKB, 6), blk(D_QKB, 7),
            blk(D_QKB, 8), blk(LANES, 0),
            pl.BlockSpec((1, LANES), lambda b, t: (0, 0)),
            pl.BlockSpec((1, D_VB), lambda b, t: (0, 0)),
        ],
        out_specs=[
            pl.BlockSpec((CHUNK, D_VB), lambda b, t: (b * nc + t, 0)),
            pl.BlockSpec((1, N_HEADS, D_QK, D_V), lambda b, t: (b, 0, 0, 0)),
            pl.BlockSpec((1, N_HEADS, D_QK), lambda b, t: (b, 0, 0)),
            pl.BlockSpec((1, 1, LANES), lambda b, t: (b, 0, 0)),
        ],
        out_shape=[
            jax.ShapeDtypeStruct((T_P, D_VB), BF16),
            jax.ShapeDtypeStruct((BATCH, N_HEADS, D_QK, D_V), F32),
            jax.ShapeDtypeStruct((BATCH, N_HEADS, D_QK), F32),
            jax.ShapeDtypeStruct((BATCH, 1, LANES), F32),
        ],
        compiler_params=_params(2),
        name="mlstm_prompt",
    )(z_main, z_main, z_main, z_main, z_main, z_main, z_if, bias, hnorm)


def _mlstm_sample_kernel(q_ref, k_ref, v0_ref, v1_ref, zo0_ref, zo1_ref, g_ref, bias_ref, hn_ref,
                         c0_ref, n0_ref, m0_ref, hg_ref, c_ref, n_ref, m_ref):
    L = DEC_SEQ
    q = q_ref[...].astype(F32)
    k = k_ref[...].astype(F32)
    vs = (v0_ref[...].astype(F32), v1_ref[...].astype(F32))
    zos = (zo0_ref[...].astype(F32), zo1_ref[...].astype(F32))
    gates = g_ref[...]
    for s in range(SAMPLE_SEQS_PER_STEP):
        rows = slice(s * L, (s + 1) * L)
        prep = _gate_prep(L, gates[rows], bias_ref[...])
        mrow = m0_ref[s]
        lane = lax.broadcasted_iota(jnp.int32, mrow.shape, 1)
        for h in range(N_HEADS):
            hv = (h % 2) * D_V
            hg, c_new, n_new, m_new = _mlstm_head(
                L, h,
                q[rows, h * D_QK:(h + 1) * D_QK], k[rows, h * D_QK:(h + 1) * D_QK],
                vs[h // 2][rows, hv:hv + D_V], zos[h // 2][rows, hv:hv + D_V],
                prep, mrow[:, h:h + 1], c0_ref[s, h], n0_ref[s, h:h + 1, :],
                hn_ref[:, h * D_V:(h + 1) * D_V])
            hg_ref[rows, h * D_V:(h + 1) * D_V] = hg
            c_ref[s, h] = c_new
            n_ref[s, h:h + 1, :] = n_new
            mrow = jnp.where(lane == h, m_new, mrow)
        m_ref[s] = mrow


def _mlstm_sample_call(z_main, z_if, bias, hnorm, c0, n0, m0):
    nb = SAMPLE_SEQS_PER_STEP
    rows = nb * DEC_SEQ
    base = T_P // rows
    blk = lambda w, c: pl.BlockSpec((rows, w), lambda g, c=c: (base + g, c))
    return pl.pallas_call(
        _mlstm_sample_kernel,
        grid=(DEC_BATCH // nb,),
        in_specs=[
            blk(D_QKB, 3), blk(D_QKB, 4), blk(D_QKB, 5), blk(D_QKB, 6), blk(D_QKB, 7),
            blk(D_QKB, 8), blk(LANES, 0),
            pl.BlockSpec((1, LANES), lambda g: (0, 0)),
            pl.BlockSpec((1, D_VB), lambda g: (0, 0)),
            pl.BlockSpec((nb, N_HEADS, D_QK, D_V), lambda g: (g, 0, 0, 0)),
            pl.BlockSpec((nb, N_HEADS, D_QK), lambda g: (g, 0, 0)),
            pl.BlockSpec((nb, 1, LANES), lambda g: (g, 0, 0)),
        ],
        out_specs=[
            pl.BlockSpec((rows, D_VB), lambda g: (g, 0)),
            pl.BlockSpec((nb, N_HEADS, D_QK, D_V), lambda g: (g, 0, 0, 0)),
            pl.BlockSpec((nb, N_HEADS, D_QK), lambda g: (g, 0, 0)),
            pl.BlockSpec((nb, 1, LANES), lambda g: (g, 0, 0)),
        ],
        out_shape=[
            jax.ShapeDtypeStruct((T_S, D_VB), F32),
            jax.ShapeDtypeStruct((DEC_BATCH, N_HEADS, D_QK, D_V), F32),
            jax.ShapeDtypeStruct((DEC_BATCH, N_HEADS, D_QK), F32),
            jax.ShapeDtypeStruct((DEC_BATCH, 1, LANES), F32),
        ],
        compiler_params=_params(1),
        name="mlstm_sample",
    )(z_main, z_main, z_main, z_main, z_main, z_main, z_if, bias, hnorm, c0, n0, m0)


NP_MIX = T_P // TM_MIX
TILES_PER_SEQ = SEQ // TM_MIX


def _mixer_kernel(cb_ref, cc_ref, cx_ref, ccp_ref, cxp_ref, buf_ref, hgp_ref, hgs_ref, ga_ref,
                  gb_ref, cw_ref, wa_ref, wb_ref, mix_ref, utail_ref, us_ref, a_ref):
    i = pl.program_id(0)
    u = cc_ref[...].astype(F32) * cx_ref[...].astype(F32)
    w0 = cw_ref[0:1, :]
    w1 = cw_ref[1:2, :]
    w2 = cw_ref[2:3, :]
    rowid = lax.broadcasted_iota(jnp.int32, u.shape, 0)
    um1 = pltpu.roll(u, 1, 0)
    um2 = pltpu.roll(u, 2, 0)
    cb = cb_ref[...].astype(F32)

    @pl.when(i < NP_MIX)
    def _():
        valid = (i % TILES_PER_SEQ != 0).astype(F32)
        uprev = ccp_ref[...].astype(F32) * cxp_ref[...].astype(F32) * valid
        p1 = uprev[15:16, :]
        p2 = uprev[14:15, :]
        m1 = jnp.where(rowid == 0, p1, um1)
        m2 = jnp.where(rowid == 0, p2, jnp.where(rowid == 1, p1, um2))
        a_ref[...] = (cb * (w0 * m2 + w1 * m1 + w2 * u)).astype(BF16)

        @pl.when(i % TILES_PER_SEQ == TILES_PER_SEQ - 1)
        def _():
            utail_ref[...] = u[TM_MIX - 8:, :]

    @pl.when(i >= NP_MIX)
    def _():
        pos = rowid % DEC_SEQ
        bp = buf_ref[...]
        m1 = jnp.where(pos >= 1, um1, pltpu.roll(bp, TM_MIX - 1, 0))
        m2 = jnp.where(pos >= 2, um2, bp)
        a_ref[...] = (cb * (w0 * m2 + w1 * m1 + w2 * u)).astype(BF16)
        us_ref[...] = u

    ya = _dot(a_ref[...], wa_ref[...])

    def finish(hg):
        yb = _dot(hg, wb_ref[...])
        mix = (jax.nn.sigmoid(ga_ref[...].astype(F32)) * ya
               + jax.nn.sigmoid(gb_ref[...].astype(F32)) * yb)
        mix_ref[...] = mix.astype(mix_ref.dtype)

    @pl.when(i < NP_MIX)
    def _():
        finish(hgp_ref[...])

    @pl.when(i >= NP_MIX)
    def _():
        finish(hgs_ref[...].astype(BF16))


def _mixer_call(z_main, z_gab, bufpad, hg_p, hg_s, conv_w, wa, wb):
    tm = TM_MIX
    pidx = lambda i: jnp.minimum(i, NP_MIX - 1)
    sidx = lambda i: jnp.maximum(i - NP_MIX, 0)
    return pl.pallas_call(
        _mixer_kernel,
        grid=(T_ALL // tm,),
        in_specs=[
            pl.BlockSpec((tm, D_CONV), lambda i: (i, 0)),
            pl.BlockSpec((tm, D_CONV), lambda i: (i, 1)),
            pl.BlockSpec((tm, D_CONV), lambda i: (i, 2)),
            pl.BlockSpec((16, D_CONV), lambda i: (jnp.maximum(i * (tm // 16) - 1, 0), 1)),
            pl.BlockSpec((16, D_CONV), lambda i: (jnp.maximum(i * (tm // 16) - 1, 0), 2)),
            pl.BlockSpec((tm, D_CONV), lambda i: (sidx(i), 0)),
            pl.BlockSpec((tm, D_VB), lambda i: (pidx(i), 0)),
            pl.BlockSpec((tm, D_VB), lambda i: (sidx(i), 0)),
            pl.BlockSpec((tm, D_MODEL), lambda i: (i, 0)),
            pl.BlockSpec((tm, D_MODEL), lambda i: (i, 1)),
            pl.BlockSpec((3, D_CONV), lambda i: (0, 0)),
            pl.BlockSpec((D_CONV, D_MODEL), lambda i: (0, 0)),
            pl.BlockSpec((D_VB, D_MODEL), lambda i: (0, 0)),
        ],
        out_specs=[
            pl.BlockSpec((tm, D_MODEL), lambda i: (i, 0)),
            pl.BlockSpec((8, D_CONV), lambda i: (jnp.minimum(i // TILES_PER_SEQ, BATCH - 1), 0)),
            pl.BlockSpec((tm, D_CONV), lambda i: (sidx(i), 0)),
        ],
        out_shape=[
            jax.ShapeDtypeStruct((T_ALL, D_MODEL), BF16),
            jax.ShapeDtypeStruct((BATCH * 8, D_CONV), F32),
            jax.ShapeDtypeStruct((T_S, D_CONV), F32),
        ],
        scratch_shapes=[pltpu.VMEM((tm, D_CONV), BF16)],
        compiler_params=_params(1),
        name="mixer",
    )(z_main, z_main, z_main, z_main, z_main, bufpad, hg_p, hg_s, z_gab, z_gab, conv_w, wa, wb)


RT_E1, RT_E2, RT_V1, RT_V2, RT_P1, RT_P2 = range(6)
ST_CNT, ST_CUR, ST_FREE = range(3)


D_PACK = D_MODEL // 2


def _pack_bf16_pairs(x):
    lo = lax.bitcast_convert_type(x[:, :D_PACK].astype(BF16).astype(F32), jnp.uint32)
    hi = lax.bitcast_convert_type(x[:, D_PACK:].astype(BF16).astype(F32), jnp.uint32)
    return hi | (lo >> 16)


def _unpack_bf16_pairs(w):
    lo = lax.bitcast_convert_type(w << 16, F32).astype(BF16)
    hi = lax.bitcast_convert_type(w & jnp.uint32(0xFFFF0000), F32).astype(BF16)
    return jnp.concatenate([lo, hi], axis=1)


def _scatter_rows(hbuf, slot, pos_sm, xs_hbm, sem):
    for k in range(TOP_K):
        for r in range(TM_MIX):
            pltpu.make_async_copy(hbuf.at[slot, pl.ds(r, 1), :],
                                  xs_hbm.at[pl.ds(pos_sm[slot, k, r], 1), :],
                                  sem.at[slot]).start()


def _scatter_wait(hbuf, slot, xs_hbm, sem):
    for _ in range(TOP_K):
        pltpu.make_async_copy(xs_hbm.at[pl.ds(0, TM_MIX), :], hbuf.at[slot], sem.at[slot]).wait()


def _router_kernel(mix_ref, xp_ref, xs_ref, wo_ref, g2_ref, wr_hi_ref, wr_lo_ref,
                   x2_ref, rt_ref, tlog_ref, xs_hbm,
                   hbuf, st_ref, pos_vm, pos_sm, st_vm, st_sm, zt_ref,
                   sem_row, sem_pos, sem_st, sem_fill):
    i = pl.program_id(0)
    n = pl.num_programs(0)
    slot = i % 2
    tm = TM_MIX

    @pl.when(i == 0)
    def _():
        st_ref[...] = jnp.zeros_like(st_ref)
        zt_ref[...] = jnp.zeros_like(zt_ref)

    @pl.when(i >= 1)
    def _():
        pltpu.make_async_copy(pos_vm.at[1 - slot], pos_sm.at[1 - slot], sem_pos.at[1 - slot]).wait()
        _scatter_rows(hbuf, 1 - slot, pos_sm, xs_hbm, sem_row)

    @pl.when(i >= 2)
    def _():
        _scatter_wait(hbuf, slot, xs_hbm, sem_row)

    x2_ref[...] = _dot(mix_ref[...], wo_ref[...])

    @pl.when(i < NP_MIX)
    def _():
        x2_ref[...] += xp_ref[...]

    @pl.when(i >= NP_MIX)
    def _():
        x2_ref[...] += xs_ref[...]

    x2 = x2_ref[...]
    hn2 = x2 * lax.rsqrt(jnp.mean(x2 * x2, axis=-1, keepdims=True) + EPS) * g2_ref[...]
    hbuf[slot] = _pack_bf16_pairs(hn2)
    hi = hn2.astype(BF16)
    lo = (hn2 - hi.astype(F32)).astype(BF16)
    logits = _dot(hi, wr_hi_ref[...]) + (_dot(lo, wr_hi_ref[...]) + _dot(hi, wr_lo_ref[...]))
    lane = lax.broadcasted_iota(jnp.int32, logits.shape, 1)
    lanef = lane.astype(F32)
    neg = -jnp.inf
    gl = jnp.where(lane < N_GROUPS, logits, neg)
    gmax = jnp.max(gl, axis=-1, keepdims=True)
    g_star = jnp.min(jnp.where(gl == gmax, lanef, float(LANES)), axis=-1, keepdims=True)
    p_grp = 1.0 / jnp.sum(jnp.exp(gl - gmax), axis=-1, keepdims=True)
    lo_lane = N_GROUPS + EXPERTS_PER_GROUP * g_star
    el = jnp.where((lanef >= lo_lane) & (lanef < lo_lane + EXPERTS_PER_GROUP), logits, neg)
    e1 = jnp.max(el, axis=-1, keepdims=True)
    i1 = jnp.min(jnp.where(el == e1, lanef, float(LANES)), axis=-1, keepdims=True)
    el2 = jnp.where(lanef == i1, neg, el)
    e2 = jnp.max(el2, axis=-1, keepdims=True)
    i2 = jnp.min(jnp.where(el2 == e2, lanef, float(LANES)), axis=-1, keepdims=True)
    t = jnp.exp(e2 - e1)
    v1 = p_grp / (1.0 + t)
    v2 = v1 * t
    id1 = i1 - N_GROUPS
    id2 = i2 - N_GROUPS

    m1 = (lanef == id1).astype(F32)
    m2 = (lanef == id2).astype(F32)
    mm = m1 + m2
    row = lax.broadcasted_iota(jnp.int32, (tm, tm), 0)
    col = lax.broadcasted_iota(jnp.int32, (tm, tm), 1)
    earlier = _dot((row > col).astype(BF16), mm.astype(BF16))
    c_row = jnp.sum(mm, axis=0, keepdims=True)
    st = st_ref[...]
    cnt = st[ST_CNT:ST_CNT + 1, :]
    cur = st[ST_CUR:ST_CUR + 1, :]
    free = st[ST_FREE:ST_FREE + 1, :]
    inv = 1.0 / TM_EXP
    q0 = jnp.floor(cnt * inv)
    rem = cnt - TM_EXP * q0
    need0 = (c_row > 0) & (rem == 0)
    need1 = (rem > 0) & (rem + c_row > TM_EXP)
    alloc = (need0 | need1).astype(F32)
    er = lax.broadcasted_iota(jnp.int32, (LANES, LANES), 0)
    ec = lax.broadcasted_iota(jnp.int32, (LANES, LANES), 1)
    before = _dot(jnp.broadcast_to(alloc, (8, LANES)).astype(BF16), (er < ec).astype(BF16))[0:1, :]
    newtile = free + before
    tile_a = jnp.where(rem > 0, cur, newtile)
    thr = TM_EXP * (q0 + 1.0)
    base_a = TM_EXP * (tile_a - q0)
    base_b = TM_EXP * (newtile - q0 - 1.0)
    rank = earlier + cnt
    posfull = rank + jnp.where(rank < thr, base_a, base_b)
    pos1 = jnp.sum(m1 * posfull, axis=-1, keepdims=True)
    pos2 = jnp.sum(m2 * posfull, axis=-1, keepdims=True)
    cnt_new = cnt + c_row
    cur_new = jnp.where(need1, newtile, tile_a)
    free_new = free + jnp.sum(alloc, axis=-1, keepdims=True)

    rt = jnp.where(lane == RT_E1, id1,
         jnp.where(lane == RT_E2, id2,
         jnp.where(lane == RT_V1, v1,
         jnp.where(lane == RT_V2, v2,
         jnp.where(lane == RT_P1, pos1, jnp.where(lane == RT_P2, pos2, 0.0))))))
    rt_ref[...] = rt
    sub = lax.broadcasted_iota(jnp.int32, (8, LANES), 0)
    st_new = jnp.where(sub == ST_CNT, cnt_new,
             jnp.where(sub == ST_CUR, cur_new, jnp.where(sub == ST_FREE, free_new, 0.0)))
    st_ref[...] = st_new
    tlog_ref[0] = jnp.where(sub == 0, newtile, jnp.where(sub == 1, alloc, 0.0))

    pick = ((sub == 0) & (lax.broadcasted_iota(jnp.int32, (8, LANES), 1) == RT_P1)) | (
        (sub == 1) & (lax.broadcasted_iota(jnp.int32, (8, LANES), 1) == RT_P2))
    pick = pick.astype(BF16)
    rt_hi = rt.astype(BF16)
    rt_lo = (rt - rt_hi.astype(F32)).astype(BF16)
    pos_vm[slot] = (_dot_nt(pick, rt_hi) + _dot_nt(pick, rt_lo)).astype(jnp.int32)
    pltpu.make_async_copy(pos_vm.at[slot], pos_sm.at[slot], sem_pos.at[slot]).start()

    @pl.when(i == n - 1)
    def _():
        pltpu.make_async_copy(pos_vm.at[slot], pos_sm.at[slot], sem_pos.at[slot]).wait()
        _scatter_rows(hbuf, slot, pos_sm, xs_hbm, sem_row)
        st_vm[...] = st_new.astype(jnp.int32)
        pltpu.make_async_copy(st_vm, st_sm, sem_st).start()
        pltpu.make_async_copy(st_vm, st_sm, sem_st).wait()

        def fill_row(p):
            return pltpu.make_async_copy(zt_ref.at[pl.ds(0, 1), :], xs_hbm.at[pl.ds(p, 1), :], sem_fill)

        def fill_tile(tile):
            return pltpu.make_async_copy(zt_ref, xs_hbm.at[pl.ds(tile * TM_EXP, TM_EXP), :], sem_fill)

        for e in range(N_EXPERTS):
            r0 = st_sm[ST_CNT, e] % TM_EXP
            n_tail = (TM_EXP - r0) % TM_EXP
            p0 = st_sm[ST_CUR, e] * TM_EXP + r0
            lax.fori_loop(0, n_tail, lambda r, c: (fill_row(p0 + r).start(), c)[1], 0)
        first_free = st_sm[ST_FREE, 0]
        lax.fori_loop(first_free, NT_EXP, lambda tl, c: (fill_tile(tl).start(), c)[1], 0)
        for e in range(N_EXPERTS):
            n_tail = (TM_EXP - st_sm[ST_CNT, e] % TM_EXP) % TM_EXP
            lax.fori_loop(0, n_tail, lambda r, c: (fill_row(0).wait(), c)[1], 0)
        lax.fori_loop(first_free, NT_EXP, lambda tl, c: (fill_tile(0).wait(), c)[1], 0)
        _scatter_wait(hbuf, 1 - slot, xs_hbm, sem_row)
        _scatter_wait(hbuf, slot, xs_hbm, sem_row)


def _router_call(mix, xp, xs, wo, g2, wr_hi, wr_lo):
    tm = TM_MIX
    n = T_ALL // tm
    pidx = lambda i: jnp.minimum(i, NP_MIX - 1)
    sidx = lambda i: jnp.maximum(i - NP_MIX, 0)
    return pl.pallas_call(
        _router_kernel,
        grid=(n,),
        in_specs=[
            pl.BlockSpec((tm, D_MODEL), lambda i: (i, 0)),
            pl.BlockSpec((tm, D_MODEL), lambda i: (pidx(i), 0)),
            pl.BlockSpec((tm, D_MODEL), lambda i: (sidx(i), 0)),
            pl.BlockSpec((D_MODEL, D_MODEL), lambda i: (0, 0)),
            pl.BlockSpec((1, D_MODEL), lambda i: (0, 0)),
            pl.BlockSpec((D_MODEL, LANES), lambda i: (0, 0)),
            pl.BlockSpec((D_MODEL, LANES), lambda i: (0, 0)),
        ],
        out_specs=[
            pl.BlockSpec((tm, D_MODEL), lambda i: (i, 0)),
            pl.BlockSpec((tm, LANES), lambda i: (i, 0)),
            pl.BlockSpec((1, 8, LANES), lambda i: (i, 0, 0)),
            pl.BlockSpec(memory_space=pl.ANY),
        ],
        out_shape=[
            jax.ShapeDtypeStruct((T_ALL, D_MODEL), F32),
            jax.ShapeDtypeStruct((T_ALL, LANES), F32),
            jax.ShapeDtypeStruct((n, 8, LANES), F32),
            jax.ShapeDtypeStruct((R_EXP, D_PACK), jnp.uint32),
        ],
        scratch_shapes=[
            pltpu.VMEM((2, tm, D_PACK), jnp.uint32),
            pltpu.VMEM((8, LANES), F32),
            pltpu.VMEM((2, 8, tm), jnp.int32),
            pltpu.SMEM((2, 8, tm), jnp.int32),
            pltpu.VMEM((8, LANES), jnp.int32),
            pltpu.SMEM((8, LANES), jnp.int32),
            pltpu.VMEM((TM_EXP, D_PACK), jnp.uint32),
            pltpu.SemaphoreType.DMA((2,)),
            pltpu.SemaphoreType.DMA((2,)),
            pltpu.SemaphoreType.DMA(()),
            pltpu.SemaphoreType.DMA(()),
        ],
        compiler_params=_params(1),
        name="router",
    )(mix, xp, xs, wo, g2, wr_hi, wr_lo)


def _row_copy(src_hbm, row, dst_ref, slot, r, sem):
    return pltpu.make_async_copy(
        src_hbm.at[pl.ds(row, 1), :], dst_ref.at[slot, pl.ds(r, 1), :], sem.at[slot])


def _gather_start(src_hbm, idx_ref, dst_ref, slot, sem, n_rows, first=0, two_queues=False):
    for r in range(first, first + n_rows):
        _row_copy(src_hbm, idx_ref[0, 0, r], dst_ref, slot, r, sem).start(
            priority=r % 2 if two_queues else 0)


def _gather_wait(src_hbm, dst_ref, slot, sem, n_rows):
    pltpu.make_async_copy(src_hbm.at[pl.ds(0, n_rows), :], dst_ref.at[slot], sem.at[slot]).wait()


def _experts_kernel(tid_ref, te_ref, na_ref, x_ref, wg_ref, wu_ref, wd_ref, o_ref):
    j = pl.program_id(0)
    n_active = na_ref[0]

    @pl.when(j < n_active)
    def _():
        x = _unpack_bf16_pairs(x_ref[...])
        a = _dot(x, wg_ref[0])
        u = _dot(x, wu_ref[0])
        act = (a * jax.nn.sigmoid(a) * u).astype(BF16)
        o_ref[...] = _dot(act, wd_ref[0])

    @pl.when(j >= n_active)
    def _():
        o_ref[...] = jnp.zeros_like(o_ref)


def _experts_call(tile_id, tile_expert, n_active, x_sorted, wg, wu, wd):
    grid_spec = pltpu.PrefetchScalarGridSpec(
        num_scalar_prefetch=3,
        grid=(NT_EXP,),
        in_specs=[
            pl.BlockSpec((TM_EXP, D_PACK), lambda j, tid, te, na: (tid[j], 0)),
            pl.BlockSpec((1, D_MODEL, D_EXPERT), lambda j, tid, te, na: (te[j], 0, 0)),
            pl.BlockSpec((1, D_MODEL, D_EXPERT), lambda j, tid, te, na: (te[j], 0, 0)),
            pl.BlockSpec((1, D_EXPERT, D_MODEL), lambda j, tid, te, na: (te[j], 0, 0)),
        ],
        out_specs=pl.BlockSpec((TM_EXP, D_MODEL), lambda j, tid, te, na: (tid[j], 0)),
    )
    return pl.pallas_call(
        _experts_kernel,
        grid_spec=grid_spec,
        out_shape=jax.ShapeDtypeStruct((R_EXP, D_MODEL), F32),
        compiler_params=_params(1),
        name="experts",
    )(tile_id, tile_expert, n_active, x_sorted, wg, wu, wd)


def _combine_kernel(p1c_ref, p1n_ref, p2c_ref, p2n_ref, eo_hbm, x2_ref, rt_ref, gf_ref,
                    yp_ref, ys_ref, buf1, buf2, sem1, sem2):
    i = pl.program_id(0)
    n = pl.num_programs(0)
    slot = i % 2

    @pl.when(i == 0)
    def _():
        _gather_start(eo_hbm, p1c_ref, buf1, 0, sem1, TM_MIX, two_queues=True)
        _gather_start(eo_hbm, p2c_ref, buf2, 0, sem2, TM_MIX, two_queues=True)

    @pl.when(i + 1 < n)
    def _():
        _gather_start(eo_hbm, p1n_ref, buf1, 1 - slot, sem1, TM_MIX, two_queues=True)
        _gather_start(eo_hbm, p2n_ref, buf2, 1 - slot, sem2, TM_MIX, two_queues=True)

    _gather_wait(eo_hbm, buf1, slot, sem1, TM_MIX)
    _gather_wait(eo_hbm, buf2, slot, sem2, TM_MIX)
    rt = rt_ref[...]
    y = x2_ref[...] + rt[:, 2:3] * buf1[slot] + rt[:, 3:4] * buf2[slot]
    out = y * lax.rsqrt(jnp.mean(y * y, axis=-1, keepdims=True) + EPS) * gf_ref[...]

    @pl.when(i < NP_MIX)
    def _():
        yp_ref[...] = out

    @pl.when(i >= NP_MIX)
    def _():
        ys_ref[...] = out


def _combine_call(pos1, pos2, eo, x2, rt, gf):
    tm = TM_MIX
    n = T_ALL // tm
    cur = lambda i: (i, 0, 0)
    nxt = lambda i: (jnp.minimum(i + 1, n - 1), 0, 0)
    smem = lambda f: pl.BlockSpec((1, 1, tm), f, memory_space=pltpu.SMEM)
    return pl.pallas_call(
        _combine_kernel,
        grid=(n,),
        in_specs=[
            smem(lambda i: (0, 0, 0)), smem(nxt), smem(lambda i: (0, 0, 0)), smem(nxt),
            pl.BlockSpec(memory_space=pl.ANY),
            pl.BlockSpec((tm, D_MODEL), lambda i: (i, 0)),
            pl.BlockSpec((tm, LANES), lambda i: (i, 0)),
            pl.BlockSpec((1, D_MODEL), lambda i: (0, 0)),
        ],
        out_specs=[
            pl.BlockSpec((tm, D_MODEL), lambda i: (jnp.minimum(i, NP_MIX - 1), 0)),
            pl.BlockSpec((tm, D_MODEL), lambda i: (jnp.maximum(i - NP_MIX, 0), 0)),
        ],
        out_shape=[
            jax.ShapeDtypeStruct((T_P, D_MODEL), F32),
            jax.ShapeDtypeStruct((T_S, D_MODEL), F32),
        ],
        scratch_shapes=[
            pltpu.VMEM((2, tm, D_MODEL), F32),
            pltpu.VMEM((2, tm, D_MODEL), F32),
            pltpu.SemaphoreType.DMA((2,)),
            pltpu.SemaphoreType.DMA((2,)),
        ],
        compiler_params=_params(1),
        name="combine",
    )(pos1, pos1, pos2, pos2, eo, x2, rt, gf)


def _tile_schedule(tlog):
    newtile = tlog[:, 0, :N_EXPERTS].astype(jnp.int32)
    alloc = tlog[:, 1, :N_EXPERTS] > 0.5
    expert = jnp.broadcast_to(jnp.arange(N_EXPERTS, dtype=jnp.int32)[None, :], newtile.shape)
    owner = jnp.full((NT_EXP,), N_EXPERTS, jnp.int32).at[
        jnp.where(alloc, newtile, NT_EXP).reshape(-1)].set(expert.reshape(-1), mode="drop")
    tile_id = jnp.argsort(owner, stable=True).astype(jnp.int32)
    owner_sorted = owner[tile_id]
    n_active = jnp.sum((owner < N_EXPERTS).astype(jnp.int32))
    last = owner_sorted[jnp.maximum(n_active - 1, 0)]
    tile_expert = jnp.where(jnp.arange(NT_EXP) < n_active, owner_sorted, last).astype(jnp.int32)
    return tile_id, tile_expert, n_active.reshape(1)


def kernel(x_prompt, x_sample, state_conv, state_C, state_n, state_m, norm1, w_in, b_if, conv_w,
           head_norm, w_a_out, w_b_out, w_out, norm2, w_grp, w_rt, w_expert_gate, w_expert_up,
           w_expert_down, norm_final):
    xp = x_prompt.reshape(T_P, D_MODEL)
    xs = x_sample.reshape(T_S, D_MODEL)
    w_in_t = jnp.swapaxes(w_in[0], 0, 1)
    w_gab = w_in_t[N_MAIN + N_IF:]
    w_if = jnp.pad(w_in_t[N_MAIN:N_MAIN + N_IF], ((0, LANES - N_IF), (0, 0)))
    bias = jnp.pad(b_if[0].astype(F32), (0, LANES - N_IF)).reshape(1, LANES)

    hn = _norm_call(xp, xs, norm1[0].reshape(1, D_MODEL))
    z_main = _proj_call(hn, w_in_t, N_MAIN, TN_PROJ, BF16, "inproj_main")
    z_gab = _proj_call(hn, w_gab, 2 * D_MODEL, TN_PROJ, BF16, "inproj_gates")
    z_if = _proj_call(hn, w_if, LANES, LANES, F32, "inproj_if")

    hnorm = head_norm[0].astype(F32).reshape(1, D_VB)
    hg_p, c_p, n_p, m_p = _mlstm_prompt_call(z_main, z_if, bias, hnorm)
    m0 = jnp.pad(state_m[0].astype(F32), ((0, 0), (0, LANES - N_HEADS))).reshape(DEC_BATCH, 1, LANES)
    hg_s, c_s, n_s, m_s = _mlstm_sample_call(
        z_main, z_if, bias, hnorm, state_C[0].astype(F32), state_n[0].astype(F32), m0)

    bufpad = jnp.pad(state_conv[0].astype(F32), ((0, 0), (0, DEC_SEQ - 2), (0, 0))).reshape(T_S, D_CONV)
    mix, utail, us = _mixer_call(z_main, z_gab, bufpad, hg_p, hg_s, conv_w[0],
                                 w_a_out[0].astype(BF16), w_b_out[0].astype(BF16))

    wr = jnp.pad(jnp.concatenate([w_grp[0], w_rt[0]], axis=1),
                 ((0, 0), (0, LANES - N_GROUPS - N_EXPERTS)))
    wr_hi = wr.astype(BF16)
    wr_lo = (wr - wr_hi.astype(F32)).astype(BF16)
    x2, rt, tlog, x_sorted = _router_call(mix, xp, xs, w_out[0].astype(BF16),
                                          norm2[0].reshape(1, D_MODEL), wr_hi, wr_lo)

    tile_id, tile_expert, n_active = _tile_schedule(tlog)
    eo = _experts_call(tile_id, tile_expert, n_active, x_sorted, w_expert_gate[0].astype(BF16),
                       w_expert_up[0].astype(BF16), w_expert_down[0].astype(BF16))
    pos = rt[:, RT_P1:RT_P2 + 1].astype(jnp.int32)
    pos1 = pos[:, 0].reshape(T_ALL // TM_MIX, 1, TM_MIX)
    pos2 = pos[:, 1].reshape(T_ALL // TM_MIX, 1, TM_MIX)
    y_p, y_s = _combine_call(pos1, pos2, eo, x2, rt, norm_final.reshape(1, D_MODEL))

    new_conv_p = utail.reshape(BATCH, 8, D_CONV)[:, 6:8, :][None]
    new_conv_s = us.reshape(DEC_BATCH, DEC_SEQ, D_CONV)[:, DEC_SEQ - 2:, :][None]
    return (y_p.reshape(BATCH, SEQ, D_MODEL), y_s.reshape(DEC_BATCH, DEC_SEQ, D_MODEL),
            new_conv_p, c_p[None], n_p[None], m_p[:, 0, :N_HEADS][None],
            new_conv_s, c_s[None], n_s[None], m_s[:, 0, :N_HEADS][None])
```
